```python
import math
import jax, jax.numpy as jnp
from jax import lax
import numpy as np

D_MODEL = 2048
BATCH = 2
SEQ = 4096
DEPTH = 1
DEC_BATCH = 32
DEC_SEQ = 4
PAST_LEN = 16384
PAGE_SIZE = 128

HEAD_DIM = 128
H_SB = 8
H_FOX = 8
W_SB = H_SB * HEAD_DIM
W_FOX = H_FOX * HEAD_DIM
D_PLE = 256
Q_BLOCK = 128
LN_EPS = 1e-5
ALPHA = (2 * DEPTH) ** 0.25
BETA = (8 * DEPTH) ** -0.25
IN_SPLITS = (W_SB, W_SB, W_SB, W_SB, W_FOX, W_FOX, W_FOX, W_FOX, H_FOX, D_MODEL, D_MODEL)
D_IN = sum(IN_SPLITS)

kernel_name = "stickbreak_fox_gated_hybrid_step"

F32 = jnp.float32


def layer_norm(x, g, b):
    xf = x.astype(F32)
    mu = jnp.mean(xf, axis=-1, keepdims=True)
    var = jnp.mean(jnp.square(xf - mu), axis=-1, keepdims=True)
    y = (xf - mu) * lax.rsqrt(var + LN_EPS)
    return (y * g.astype(F32) + b.astype(F32)).astype(x.dtype)


def project_inputs(h, w_in, b_f):
    u = h @ w_in
    idx = np.cumsum(IN_SPLITS)[:-1].tolist()
    q_sb, k_sb, v_sb, z_sb, q_fx, k_fx, v_fx, z_fx, f_logit, g_sb, g_fx = jnp.split(u, idx, axis=-1)
    heads = lambda a, n: a.reshape(*a.shape[:-1], n, HEAD_DIM)
    logf = jax.nn.log_sigmoid((f_logit + b_f).astype(F32))
    return (heads(q_sb, H_SB), heads(k_sb, H_SB), heads(v_sb, H_SB), z_sb,
            heads(q_fx, H_FOX), heads(k_fx, H_FOX), heads(v_fx, H_FOX), z_fx,
            logf, g_sb, g_fx)


def sweep_query_blocks(fn, per_query, q_pos):
    T = q_pos.shape[0]
    blk = min(Q_BLOCK, T)
    nb = T // blk

    def split(a):
        return jnp.moveaxis(a.reshape(a.shape[0], nb, blk, *a.shape[2:]), 1, 0)

    out = lax.map(lambda args: fn(args[0], args[1]),
                  (tuple(split(a) for a in per_query), q_pos.reshape(nb, blk)))
    out = jnp.moveaxis(out, 0, 1)
    return out.reshape(out.shape[0], T, *out.shape[3:])


def sb_block(q, k, v, q_pos, k_pos):
    scale = 1.0 / math.sqrt(HEAD_DIM)
    z = jnp.einsum("bqhd,bkhd->bhqk", q.astype(F32), k.astype(F32)) * scale
    mask = k_pos[None, :] < q_pos[:, None]
    log_keep = jnp.where(mask, jax.nn.log_sigmoid(-z), 0.0)
    later = lax.cumsum(log_keep, axis=log_keep.ndim - 1, reverse=True) - log_keep
    a = jnp.where(mask, jnp.exp(jax.nn.log_sigmoid(z) + later), 0.0)
    o = jnp.einsum("bhqk,bkhd->bqhd", a, v.astype(F32))
    return o.astype(v.dtype)


def fox_block(q, k, v, f_q, f_k, q_pos, k_pos):
    scale = 1.0 / math.sqrt(HEAD_DIM)
    z = jnp.einsum("bqhd,bkhd->bhqk", q.astype(F32), k.astype(F32)) * scale
    bias = jnp.moveaxis(f_q, -1, 1)[..., :, None] - jnp.moveaxis(f_k, -1, 1)[..., None, :]
    mask = k_pos[None, :] <= q_pos[:, None]
    logits = jnp.where(mask, z + bias, -jnp.inf)
    p = jax.nn.softmax(logits, axis=-1)
    o = jnp.einsum("bhqk,bkhd->bqhd", p, v.astype(F32))
    return o.astype(v.dtype)


def prompt_attention(q_sb, k_sb, v_sb, q_fx, k_fx, v_fx, logf):
    T = q_sb.shape[1]
    pos = jnp.arange(T, dtype=jnp.int32)
    o_sb = sweep_query_blocks(lambda pq, qp: sb_block(pq[0], k_sb, v_sb, qp, pos), (q_sb,), pos)
    f_cum = lax.cumsum(logf, axis=1)
    o_fx = sweep_query_blocks(lambda pq, qp: fox_block(pq[0], k_fx, v_fx, pq[1], f_cum, qp, pos),
                              (q_fx, f_cum), pos)
    return o_sb, o_fx


def sample_attention(q_sb, k_sb, v_sb, q_fx, k_fx, v_fx, logf, page_table,
                     c_sb_k, c_sb_v, c_fx_k, c_fx_v, c_fx_logf):
    def per_seq(args):
        qs, ks, vs, qf, kf, vf, lf, pages = args

        def past(c):
            return c[pages].reshape(-1, *c.shape[2:])

        ks_all = jnp.concatenate([past(c_sb_k).astype(ks.dtype), ks], axis=0)[None]
        vs_all = jnp.concatenate([past(c_sb_v).astype(vs.dtype), vs], axis=0)[None]
        kf_all = jnp.concatenate([past(c_fx_k).astype(kf.dtype), kf], axis=0)[None]
        vf_all = jnp.concatenate([past(c_fx_v).astype(vf.dtype), vf], axis=0)[None]
        L = ks_all.shape[1]
        Q = qs.shape[0]
        k_pos = jnp.arange(L, dtype=jnp.int32)
        q_pos = k_pos[L - Q:]
        o_sb = sweep_query_blocks(lambda pq, qp: sb_block(pq[0], ks_all, vs_all, qp, k_pos),
                                  (qs[None],), q_pos)[0]
        f_cum = lax.cumsum(jnp.concatenate([past(c_fx_logf).astype(F32), lf], axis=0), axis=0)[None]
        o_fx = sweep_query_blocks(lambda pq, qp: fox_block(pq[0], kf_all, vf_all, pq[1], f_cum, qp, k_pos),
                                  (qf[None], f_cum[:, L - Q:]), q_pos)[0]
        return o_sb, o_fx

    return lax.map(per_seq, (q_sb, k_sb, v_sb, q_fx, k_fx, v_fx, logf, page_table))


def merge_residual(h, o_sb, o_fx, z_sb, z_fx, g_sb, g_fx, b_merge, w_br_sb, w_br_fox, w_o, ln_g, ln_b):
    u_sb = (o_sb.reshape(*o_sb.shape[:-2], W_SB) * jax.nn.silu(z_sb)) @ w_br_sb
    u_fx = (o_fx.reshape(*o_fx.shape[:-2], W_FOX) * jax.nn.silu(z_fx)) @ w_br_fox
    merged = jax.nn.sigmoid(g_sb + b_merge[0]) * u_sb + jax.nn.sigmoid(g_fx + b_merge[1]) * u_fx
    return layer_norm(ALPHA * h + merged @ w_o, ln_g, ln_b)


def ple_residual(h, p, w_pe, w_pg, b_pg, ln_g, ln_b):
    gate = jax.nn.sigmoid(h @ w_pg + b_pg)
    e = p.astype(h.dtype) @ w_pe
    return layer_norm(ALPHA * h + gate * e, ln_g, ln_b)


def setup_inputs(seed: int = 0) -> dict:
    key = jax.random.key(seed)
    ks = jax.random.split(key, 26)
    n_pages = PAST_LEN // PAGE_SIZE
    n_pool = (DEC_BATCH * n_pages * 5) // 4

    def nrm(k, shape, s):
        return jax.random.normal(k, shape, F32) * s

    col_scale = jnp.concatenate([
        jnp.full((n,), BETA if j in (2, 6) else 1.0, F32) for j, n in enumerate(IN_SPLITS)])
    return {
        "x_prompt": nrm(ks[0], (BATCH, SEQ, D_MODEL), 1.0),
        "x_sample": nrm(ks[1], (DEC_BATCH, DEC_SEQ, D_MODEL), 1.0),
        "cache_sb_k": nrm(ks[2], (DEPTH, n_pool, PAGE_SIZE, H_SB, HEAD_DIM), 1.0),
        "cache_sb_v": nrm(ks[3], (DEPTH, n_pool, PAGE_SIZE, H_SB, HEAD_DIM), BETA),
        "cache_fox_k": nrm(ks[4], (DEPTH, n_pool, PAGE_SIZE, H_FOX, HEAD_DIM), 1.0),
        "cache_fox_v": nrm(ks[5], (DEPTH, n_pool, PAGE_SIZE, H_FOX, HEAD_DIM), BETA),
        "cache_fox_logf": jax.nn.log_sigmoid(2.5 + nrm(ks[6], (DEPTH, n_pool, PAGE_SIZE, H_FOX), 0.5)),
        "page_table": jax.random.permutation(ks[7], n_pool)[:DEC_BATCH * n_pages]
                         .reshape(DEC_BATCH, n_pages).astype(jnp.int32),
        "p_prompt": nrm(ks[8], (DEPTH, BATCH, SEQ, D_PLE), 1.0),
        "p_sample": nrm(ks[9], (DEPTH, DEC_BATCH, DEC_SEQ, D_PLE), 1.0),
        "ln_in_g": 1.0 + nrm(ks[10], (D_MODEL,), 0.02),
        "ln_in_b": nrm(ks[11], (D_MODEL,), 0.02),
        "w_in": nrm(ks[12], (DEPTH, D_MODEL, D_IN), D_MODEL ** -0.5) * col_scale,
        "b_f": jax.random.uniform(ks[13], (DEPTH, H_FOX), F32, minval=1.0, maxval=4.0),
        "b_merge": nrm(ks[14], (DEPTH, 2, D_MODEL), 0.02),
        "w_br_sb": nrm(ks[15], (DEPTH, W_SB, D_MODEL), BETA * W_SB ** -0.5),
        "w_br_fox": nrm(ks[16], (DEPTH, W_FOX, D_MODEL), BETA * W_FOX ** -0.5),
        "w_o": nrm(ks[17], (DEPTH, D_MODEL, D_MODEL), BETA * D_MODEL ** -0.5),
        "ln_mix_g": 1.0 + nrm(ks[18], (DEPTH, D_MODEL), 0.02),
        "ln_mix_b": nrm(ks[19], (DEPTH, D_MODEL), 0.02),
        "w_pe": nrm(ks[20], (DEPTH, D_PLE, D_MODEL), BETA * D_PLE ** -0.5),
        "w_pg": nrm(ks[21], (DEPTH, D_MODEL, D_MODEL), D_MODEL ** -0.5),
        "b_pg": nrm(ks[22], (DEPTH, D_MODEL), 0.02),
        "ln_ple_g": 1.0 + nrm(ks[23], (DEPTH, D_MODEL), 0.02),
        "ln_ple_b": nrm(ks[24], (DEPTH, D_MODEL), 0.02),
    }


def reference(x_prompt, x_sample, cache_sb_k, cache_sb_v, cache_fox_k, cache_fox_v, cache_fox_logf,
              page_table, p_prompt, p_sample, ln_in_g, ln_in_b, w_in, b_f, b_merge, w_br_sb, w_br_fox,
              w_o, ln_mix_g, ln_mix_b, w_pe, w_pg, b_pg, ln_ple_g, ln_ple_b):
    hp = layer_norm(x_prompt, ln_in_g, ln_in_b)
    hs = layer_norm(x_sample, ln_in_g, ln_in_b)
    st_p = ([], [], [], [], [])
    st_s = ([], [], [], [], [])
    for i in range(DEPTH):
        q_sb, k_sb, v_sb, z_sb, q_fx, k_fx, v_fx, z_fx, logf, g_sb, g_fx = project_inputs(hp, w_in[i], b_f[i])
        o_sb, o_fx = prompt_attention(q_sb, k_sb, v_sb, q_fx, k_fx, v_fx, logf)
        hp = merge_residual(hp, o_sb, o_fx, z_sb, z_fx, g_sb, g_fx, b_merge[i], w_br_sb[i], w_br_fox[i],
                            w_o[i], ln_mix_g[i], ln_mix_b[i])
        hp = ple_residual(hp, p_prompt[i], w_pe[i], w_pg[i], b_pg[i], ln_ple_g[i], ln_ple_b[i])
        for lst, a in zip(st_p, (k_sb, v_sb, k_fx, v_fx, logf)):
            lst.append(a)
        q_sb, k_sb, v_sb, z_sb, q_fx, k_fx, v_fx, z_fx, logf, g_sb, g_fx = project_inputs(hs, w_in[i], b_f[i])
        o_sb, o_fx = sample_attention(q_sb, k_sb, v_sb, q_fx, k_fx, v_fx, logf, page_table,
                                      cache_sb_k[i], cache_sb_v[i], cache_fox_k[i], cache_fox_v[i],
                                      cache_fox_logf[i])
        hs = merge_residual(hs, o_sb, o_fx, z_sb, z_fx, g_sb, g_fx, b_merge[i], w_br_sb[i], w_br_fox[i],
                            w_o[i], ln_mix_g[i], ln_mix_b[i])
        hs = ple_residual(hs, p_sample[i], w_pe[i], w_pg[i], b_pg[i], ln_ple_g[i], ln_ple_b[i])
        for lst, a in zip(st_s, (k_sb, v_sb, k_fx, v_fx, logf)):
            lst.append(a)
    return (hp, hs,
            jnp.stack(st_p[0]), jnp.stack(st_p[1]), jnp.stack(st_p[2]), jnp.stack(st_p[3]), jnp.stack(st_p[4]),
            jnp.stack(st_s[0]), jnp.stack(st_s[1]), jnp.stack(st_s[2]), jnp.stack(st_s[3]), jnp.stack(st_s[4]))
```

```python
import functools
import math

import jax
import jax.numpy as jnp
from jax import lax
from jax.experimental import pallas as pl
from jax.experimental.pallas import tpu as pltpu

F32 = jnp.float32
BF16 = jnp.bfloat16

HEAD_DIM = 128
N_HEADS = 8
HEAD_W = N_HEADS * HEAD_DIM
LN_EPS = 1e-5
QK_SCALE = 1.0 / math.sqrt(HEAD_DIM)
NEG_BIG = -1e30
V7X_VMEM_LIMIT = 56 * 1024 * 1024

NT_DIMS = (((1,), (1,)), ((), ()))


def _nt_dot(a, b):
    return lax.dot_general(a, b, NT_DIMS, preferred_element_type=F32)


def _dot(a, b):
    return jnp.dot(a, b, preferred_element_type=F32)


def _log_sigmoid(x):
    return jnp.minimum(x, 0.0) - jnp.log1p(jnp.exp(-jnp.abs(x)))


def _split2(x):
    hi = x.astype(BF16)
    lo = (x - hi.astype(F32)).astype(BF16)
    return hi, lo


def _split3(x):
    hi = x.astype(BF16)
    r = x - hi.astype(F32)
    mid = r.astype(BF16)
    lo = (r - mid.astype(F32)).astype(BF16)
    return hi, mid, lo


def _layer_norm(x, g, b):
    mu = jnp.mean(x, axis=-1, keepdims=True)
    xc = x - mu
    var = jnp.mean(xc * xc, axis=-1, keepdims=True)
    return xc * lax.rsqrt(var + LN_EPS) * g + b


def _pick_tile(n, limit):
    if n <= HEAD_DIM:
        return n
    best = None
    for t in range(HEAD_DIM, min(n, limit) + 1, HEAD_DIM):
        if n % t == 0:
            best = t
    assert best is not None, (n, limit)
    return best


def _params(sem):
    return pltpu.CompilerParams(dimension_semantics=sem, vmem_limit_bytes=V7X_VMEM_LIMIT)


def _upper_strict(n):
    row = lax.broadcasted_iota(jnp.int32, (n, n), 0)
    col = lax.broadcasted_iota(jnp.int32, (n, n), 1)
    return jnp.where(row > col, 1.0, 0.0).astype(BF16)


def _rev_excl_cumsum(x, tri):
    rows = x.shape[0]
    hi, lo = _split2(x)
    both = _dot(jnp.concatenate([hi, lo], axis=0), tri)
    return both[:rows] + both[rows:]


def _ln_proj_body(*refs, emit16, aux):
    x_ref, g_ref, b_ref, w_ref = refs[:4]
    pos = 4
    if aux:
        wf_ref, bf_ref = refs[pos:pos + 2]
        pos += 2
    y32_ref = refs[pos]
    pos += 1
    if emit16:
        y16_ref = refs[pos]
        pos += 1
    if aux:
        logf_ref = refs[pos]
        pos += 1
    xs_ref = refs[pos]

    @pl.when(pl.program_id(1) == 0)
    def _():
        h = _layer_norm(x_ref[...], g_ref[...], b_ref[...])
        xs_ref[...] = h.astype(BF16)
        if aux:
            f = _dot(xs_ref[...], wf_ref[...]) + bf_ref[...]
            logf_ref[...] = _log_sigmoid(f)

    y = _dot(xs_ref[...], w_ref[...])
    y32_ref[...] = y
    if emit16:
        y16_ref[...] = y.astype(BF16)


def _ln_proj(x, g, b, w16, *, emit16, wf16=None, bf=None):
    m, d = x.shape
    n = w16.shape[1]
    tm = _pick_tile(m, 512)
    tn = _pick_tile(n, 1024)
    aux = wf16 is not None
    in_specs = [
        pl.BlockSpec((tm, d), lambda i, j: (i, 0)),
        pl.BlockSpec((1, d), lambda i, j: (0, 0)),
        pl.BlockSpec((1, d), lambda i, j: (0, 0)),
        pl.BlockSpec((d, tn), lambda i, j: (0, j)),
    ]
    args = [x, g.reshape(1, d), b.reshape(1, d), w16]
    out_shape = [jax.ShapeDtypeStruct((m, n), F32)]
    out_specs = [pl.BlockSpec((tm, tn), lambda i, j: (i, j))]
    if emit16:
        out_shape.append(jax.ShapeDtypeStruct((m, n), BF16))
        out_specs.append(pl.BlockSpec((tm, tn), lambda i, j: (i, j)))
    if aux:
        in_specs += [pl.BlockSpec((d, HEAD_DIM), lambda i, j: (0, 0)),
                     pl.BlockSpec((1, HEAD_DIM), lambda i, j: (0, 0))]
        args += [wf16, bf]
        out_shape.append(jax.ShapeDtypeStruct((m, HEAD_DIM), F32))
        out_specs.append(pl.BlockSpec((tm, HEAD_DIM), lambda i, j: (i, 0)))
    return pl.pallas_call(
        functools.partial(_ln_proj_body, emit16=emit16, aux=aux),
        grid=(m // tm, n // tn),
        in_specs=in_specs,
        out_specs=out_specs,
        out_shape=out_shape,
        scratch_shapes=[pltpu.VMEM((tm, d), BF16)],
        compiler_params=_params(("parallel", "arbitrary")),
        name="ln_proj",
    )(*args)


def _fcum_body(lf_ref, fq_ref, frow_ref, carry_ref, *, tc):
    @pl.when(pl.program_id(1) == 0)
    def _():
        carry_ref[...] = jnp.zeros_like(carry_ref)

    row = lax.broadcasted_iota(jnp.int32, (tc, tc), 0)
    col = lax.broadcasted_iota(jnp.int32, (tc, tc), 1)
    lower_incl = jnp.where(col <= row, 1.0, 0.0).astype(BF16)
    hi, mid, lo = _split3(lf_ref[...])
    f = _dot(lower_incl, hi) + _dot(lower_incl, mid) + _dot(lower_incl, lo) + carry_ref[...]
    fq_ref[...] = f
    carry_ref[...] = f[tc - 1:tc, :]
    sel = jnp.where(lax.broadcasted_iota(jnp.int32, (N_HEADS, HEAD_DIM), 0)
                    == lax.broadcasted_iota(jnp.int32, (N_HEADS, HEAD_DIM), 1), 1.0, 0.0).astype(BF16)
    fh, fm, fl = _split3(f)
    frow_ref[...] = _nt_dot(sel, fh) + _nt_dot(sel, fm) + _nt_dot(sel, fl)


def _fcum(logf, batch, seq):
    tc = min(256, seq)
    nc = seq // tc
    return pl.pallas_call(
        functools.partial(_fcum_body, tc=tc),
        grid=(batch, nc),
        in_specs=[pl.BlockSpec((tc, HEAD_DIM), lambda b, c: (b * nc + c, 0))],
        out_specs=[pl.BlockSpec((tc, HEAD_DIM), lambda b, c: (b * nc + c, 0)),
                   pl.BlockSpec((None, N_HEADS, tc), lambda b, c: (b, 0, c))],
        out_shape=[jax.ShapeDtypeStruct((batch * seq, HEAD_DIM), F32),
                   jax.ShapeDtypeStruct((batch, N_HEADS, seq), F32)],
        scratch_shapes=[pltpu.VMEM((1, HEAD_DIM), F32)],
        compiler_params=_params(("parallel", "arbitrary")),
        name="fcum",
    )(logf)


def _sb_prompt_body(q_ref, k_ref, v_ref, o_ref, *, tq):
    i = pl.program_id(2)
    q = q_ref[...]
    tri = _upper_strict(tq)
    row = lax.broadcasted_iota(jnp.int32, (tq, tq), 0)
    col = lax.broadcasted_iota(jnp.int32, (tq, tq), 1)
    causal = col < row

    def block(j, carry, masked):
        acc, c = carry
        start = pl.multiple_of(j * tq, tq)
        k = k_ref[pl.ds(start, tq), :]
        v = v_ref[pl.ds(start, tq), :]
        z = _nt_dot(q, k) * QK_SCALE
        lsn = _log_sigmoid(-z)
        lk = jnp.where(causal, lsn, 0.0) if masked else lsn
        later = _rev_excl_cumsum(lk, tri) + c
        a = jnp.exp(z + lsn + later)
        if masked:
            a = jnp.where(causal, a, 0.0)
        acc = acc + _dot(a.astype(BF16), v)
        c = c + jnp.sum(lk, axis=-1, keepdims=True)
        return acc, c

    carry = (jnp.zeros((tq, HEAD_DIM), F32), jnp.zeros((tq, 1), F32))
    carry = block(i, carry, True)
    acc, _ = lax.fori_loop(0, i, lambda jj, cr: block(i - 1 - jj, cr, False), carry)
    o_ref[...] = acc


def _fox_prompt_body(q_ref, k_ref, v_ref, fq_ref, frow_ref, o_ref, *, tq):
    h = pl.program_id(1)
    i = pl.program_id(2)
    q = q_ref[...]
    lane = lax.broadcasted_iota(jnp.int32, (tq, HEAD_DIM), 1)
    fq = jnp.sum(jnp.where(lane == h, fq_ref[...], 0.0), axis=-1, keepdims=True)
    row = lax.broadcasted_iota(jnp.int32, (tq, tq), 0)
    col = lax.broadcasted_iota(jnp.int32, (tq, tq), 1)
    causal = col <= row

    def block(j, carry, masked):
        acc, m, l = carry
        start = pl.multiple_of(j * tq, tq)
        k = k_ref[pl.ds(start, tq), :]
        v = v_ref[pl.ds(start, tq), :]
        fk = frow_ref[h, pl.ds(j, 1), :]
        logits = _nt_dot(q, k) * QK_SCALE + (fq - fk)
        if masked:
            logits = jnp.where(causal, logits, NEG_BIG)
        m_new = jnp.maximum(m, jnp.max(logits, axis=-1, keepdims=True))
        alpha = jnp.exp(m - m_new)
        p = jnp.exp(logits - m_new)
        l = l * alpha + jnp.sum(p, axis=-1, keepdims=True)
        acc = acc * alpha + _dot(p.astype(BF16), v)
        return acc, m_new, l

    carry = (jnp.zeros((tq, HEAD_DIM), F32), jnp.full((tq, 1), NEG_BIG, F32), jnp.zeros((tq, 1), F32))
    carry = block(i, carry, True)
    acc, _, l = lax.fori_loop(0, i, lambda jj, cr: block(i - 1 - jj, cr, False), carry)
    o_ref[...] = acc / l


def _prompt_attention(qkv16, fq, frow, batch, seq):
    tq = min(256, seq)
    nq = seq // tq
    m = batch * seq
    grid = (batch, N_HEADS, nq)

    def q_spec(group):
        return pl.BlockSpec((tq, HEAD_DIM), lambda b, h, i: (b * nq + i, group * N_HEADS + h))

    def kv_spec(group):
        return pl.BlockSpec((seq, HEAD_DIM), lambda b, h, i: (b, group * N_HEADS + h))

    o_spec = pl.BlockSpec((tq, HEAD_DIM), lambda b, h, i: (b * nq + i, h))
    o_shape = jax.ShapeDtypeStruct((m, HEAD_W), F32)
    sem = ("parallel", "parallel", "arbitrary")
    o_sb = pl.pallas_call(
        functools.partial(_sb_prompt_body, tq=tq),
        grid=grid,
        in_specs=[q_spec(0), kv_spec(1), kv_spec(2)],
        out_specs=o_spec, out_shape=o_shape,
        compiler_params=_params(sem), name="sb_prompt",
    )(qkv16, qkv16, qkv16)
    o_fx = pl.pallas_call(
        functools.partial(_fox_prompt_body, tq=tq),
        grid=grid,
        in_specs=[q_spec(3), kv_spec(4), kv_spec(5),
                  pl.BlockSpec((tq, HEAD_DIM), lambda b, h, i: (b * nq + i, 0)),
                  pl.BlockSpec((None, N_HEADS, nq, tq), lambda b, h, i: (b, 0, 0, 0))],
        out_specs=o_spec, out_shape=o_shape,
        compiler_params=_params(sem), name="fox_prompt",
    )(qkv16, qkv16, qkv16, fq, frow.reshape(batch, N_HEADS, nq, tq))
    return o_sb, o_fx


Q_PAD = 8
N_PAIRS = N_HEADS // 2
PAIR_W = 2 * HEAD_DIM


def _expand_heads(x):
    n = x.shape[1]
    return jnp.concatenate([jnp.broadcast_to(x[h:h + 1, :], (Q_PAD, n)) for h in range(N_HEADS)], axis=0)


def _sample_body(pt_ref, *refs, n_new, pages_per_step, page, chunk):
    del pt_ref
    g = pages_per_step
    (qs_ref, qf_ref, ksn_ref, vsn_ref, kfn_ref, vfn_ref, lfr_ref, tri_ref) = refs[:8]
    pos = 8
    csk = refs[pos:pos + g]; pos += g
    csv = refs[pos:pos + g]; pos += g
    cfk = refs[pos:pos + g]; pos += g
    cfv = refs[pos:pos + g]; pos += g
    clf = refs[pos:pos + g]; pos += g
    osb_ref, ofx_ref = refs[pos:pos + 2]; pos += 2
    (qbs_ref, qbf_ref, accs_ref, accf_ref, cs_ref, m_ref, l_ref, cg_ref, cq_ref, pad_ref) = refs[pos:]

    rows = N_HEADS * Q_PAD
    p = pl.program_id(1)
    n_steps = pl.num_programs(1)

    def pair_rows(x, pp):
        return x[pp * 2 * Q_PAD:(pp + 1) * 2 * Q_PAD]

    def scores(qb_ref, key_pair):
        return jnp.concatenate(
            [_nt_dot(qb_ref[pp], key_pair(pp)) for pp in range(N_PAIRS)], axis=0) * QK_SCALE

    @pl.when(p == 0)
    def _init():
        zeros = jnp.zeros((Q_PAD, HEAD_DIM), F32)

        def block_diag(q_ref_, qb_ref):
            for pp in range(N_PAIRS):
                top = jnp.concatenate([q_ref_[(2 * pp) * Q_PAD:(2 * pp + 1) * Q_PAD, :], zeros], axis=1)
                bot = jnp.concatenate([zeros, q_ref_[(2 * pp + 1) * Q_PAD:(2 * pp + 2) * Q_PAD, :]], axis=1)
                qb_ref[pp] = jnp.concatenate([top, bot], axis=0).astype(BF16)

        block_diag(qs_ref, qbs_ref)
        block_diag(qf_ref, qbf_ref)

        kw = pad_ref.shape[0]
        qi = lax.broadcasted_iota(jnp.int32, (rows, kw), 0) & (Q_PAD - 1)
        ki = lax.broadcasted_iota(jnp.int32, (rows, kw), 1)
        tri = tri_ref[0:kw, 0:kw]

        def new_pair(ref):
            def build(pp):
                pad_ref[...] = jnp.zeros_like(pad_ref)
                pad_ref[0:n_new, 0:HEAD_DIM] = ref[2 * pp]
                pad_ref[0:n_new, HEAD_DIM:PAIR_W] = ref[2 * pp + 1]
                return pad_ref[...].astype(BF16)
            return build

        mask = ki < qi
        z = scores(qbs_ref, new_pair(ksn_ref))
        lsn = _log_sigmoid(-z)
        lk = jnp.where(mask, lsn, 0.0)
        later = _rev_excl_cumsum(lk, tri)
        a = jnp.where(mask, jnp.exp(z + lsn + later), 0.0).astype(BF16)
        v_pair = new_pair(vsn_ref)
        for pp in range(N_PAIRS):
            accs_ref[pp] = _dot(pair_rows(a, pp), v_pair(pp))
        cs_ref[...] = jnp.sum(lk, axis=-1, keepdims=True)

        lfr = lfr_ref[...]
        ck = jnp.sum(lfr, axis=-1, keepdims=True) - _rev_excl_cumsum(lfr, tri)
        ck = _expand_heads(ck)
        cq = jnp.sum(jnp.where(ki == qi, ck, 0.0), axis=-1, keepdims=True)
        cq_ref[...] = cq
        logits = scores(qbf_ref, new_pair(kfn_ref)) + (cq - ck)
        logits = jnp.where(ki <= qi, logits, NEG_BIG)
        m0 = jnp.max(logits, axis=-1, keepdims=True)
        pr = jnp.exp(logits - m0)
        m_ref[...] = m0
        l_ref[...] = jnp.sum(pr, axis=-1, keepdims=True)
        pr = pr.astype(BF16)
        v_pair = new_pair(vfn_ref)
        for pp in range(N_PAIRS):
            accf_ref[pp] = _dot(pair_rows(pr, pp), v_pair(pp))
        cg_ref[...] = jnp.zeros_like(cg_ref)

    tri = tri_ref[...]
    pages_per_chunk = chunk // page
    n_chunks = (g * page) // chunk

    for ci in reversed(range(n_chunks)):
        slots = range(ci * pages_per_chunk, (ci + 1) * pages_per_chunk)

        def cache_pair(cache_refs, slots=slots):
            def head_rows(ref, h):
                return ref[pl.ds(h, page, stride=N_HEADS), :]

            def build(pp):
                return jnp.concatenate(
                    [jnp.concatenate([head_rows(cache_refs[s], 2 * pp), head_rows(cache_refs[s], 2 * pp + 1)],
                                     axis=1)
                     for s in slots], axis=0).astype(BF16)
            return build

        z = scores(qbs_ref, cache_pair(csk))
        lsn = _log_sigmoid(-z)
        later = _rev_excl_cumsum(lsn, tri) + cs_ref[...]
        a = jnp.exp(z + lsn + later).astype(BF16)
        v_pair = cache_pair(csv)
        for pp in range(N_PAIRS):
            accs_ref[pp] += _dot(pair_rows(a, pp), v_pair(pp))
        cs_ref[...] += jnp.sum(lsn, axis=-1, keepdims=True)

        lf = jnp.concatenate([clf[s][...] for s in slots], axis=1)
        gl = _rev_excl_cumsum(lf, tri) + cg_ref[...]
        logits = scores(qbf_ref, cache_pair(cfk)) + (_expand_heads(gl) + cq_ref[...])
        m_old = m_ref[...]
        m_new = jnp.maximum(m_old, jnp.max(logits, axis=-1, keepdims=True))
        alpha = jnp.exp(m_old - m_new)
        pr = jnp.exp(logits - m_new)
        m_ref[...] = m_new
        l_ref[...] = l_ref[...] * alpha + jnp.sum(pr, axis=-1, keepdims=True)
        pr = pr.astype(BF16)
        v_pair = cache_pair(cfv)
        for pp in range(N_PAIRS):
            accf_ref[pp] = accf_ref[pp] * pair_rows(alpha, pp) + _dot(pair_rows(pr, pp), v_pair(pp))
        cg_ref[...] += jnp.sum(lf, axis=-1, keepdims=True)

    @pl.when(p == n_steps - 1)
    def _fin():
        inv_l = 1.0 / l_ref[...]
        for h in range(N_HEADS):
            pp, odd = divmod(h, 2)
            r0, c0 = odd * Q_PAD, odd * HEAD_DIM
            osb_ref[h * Q_PAD:(h + 1) * Q_PAD, :] = accs_ref[pp, r0:r0 + Q_PAD, c0:c0 + HEAD_DIM]
            ofx_ref[h * Q_PAD:(h + 1) * Q_PAD, :] = (accf_ref[pp, r0:r0 + Q_PAD, c0:c0 + HEAD_DIM]
                                                     * inv_l[h * Q_PAD:(h + 1) * Q_PAD, :])


def _sample_attention(page_table, q_sb, q_fx, k_sb, v_sb, k_fx, v_fx, lf_row,
                      c_sb_k, c_sb_v, c_fx_k, c_fx_v, c_lf_t, *, pages_per_step):
    n_seq, _, n_new, _ = k_sb.shape
    n_pages = page_table.shape[1]
    page = c_sb_k.shape[1] // N_HEADS
    g = pages_per_step
    assert n_pages % g == 0 and n_new <= Q_PAD
    n_steps = n_pages // g
    rows = N_HEADS * Q_PAD
    chunk = min(2 * page, g * page)
    assert (g * page) % chunk == 0
    tri = _upper_strict(chunk)

    def seq_spec(shape):
        return pl.BlockSpec((None,) + shape, lambda s, p, pt: (s,) + (0,) * len(shape))

    def page_index(s, p, pt, slot):
        return pt[s, (n_steps - 1 - p) * g + slot]

    def cache_spec(slot):
        return pl.BlockSpec((None, page * N_HEADS, HEAD_DIM),
                            lambda s, p, pt: (page_index(s, p, pt, slot), 0, 0))

    def logf_spec(slot):
        return pl.BlockSpec((None, N_HEADS, page), lambda s, p, pt: (page_index(s, p, pt, slot), 0, 0))

    in_specs = ([seq_spec((rows, HEAD_DIM))] * 2 + [seq_spec((N_HEADS, n_new, HEAD_DIM))] * 4
                + [seq_spec((N_HEADS, HEAD_DIM)), pl.BlockSpec((chunk, chunk), lambda s, p, pt: (0, 0))])
    args = [q_sb, q_fx, k_sb, v_sb, k_fx, v_fx, lf_row, tri]
    for cache in (c_sb_k, c_sb_v, c_fx_k, c_fx_v):
        in_specs += [cache_spec(slot) for slot in range(g)]
        args += [cache] * g
    in_specs += [logf_spec(slot) for slot in range(g)]
    args += [c_lf_t] * g

    grid_spec = pltpu.PrefetchScalarGridSpec(
        num_scalar_prefetch=1,
        grid=(n_seq, n_steps),
        in_specs=in_specs,
        out_specs=[seq_spec((rows, HEAD_DIM))] * 2,
        scratch_shapes=[
            pltpu.VMEM((N_PAIRS, 2 * Q_PAD, PAIR_W), BF16), pltpu.VMEM((N_PAIRS, 2 * Q_PAD, PAIR_W), BF16),
            pltpu.VMEM((N_PAIRS, 2 * Q_PAD, PAIR_W), F32), pltpu.VMEM((N_PAIRS, 2 * Q_PAD, PAIR_W), F32),
            pltpu.VMEM((rows, 1), F32), pltpu.VMEM((rows, 1), F32), pltpu.VMEM((rows, 1), F32),
            pltpu.VMEM((N_HEADS, 1), F32), pltpu.VMEM((rows, 1), F32),
            pltpu.VMEM((HEAD_DIM, PAIR_W), F32),
        ],
    )
    return pl.pallas_call(
        functools.partial(_sample_body, n_new=n_new, pages_per_step=g, page=page, chunk=chunk),
        grid_spec=grid_spec,
        out_shape=[jax.ShapeDtypeStruct((n_seq, rows, HEAD_DIM), F32)] * 2,
        compiler_params=_params(("parallel", "arbitrary")),
        name="sample_attn",
    )(page_table, *args)


def _merge_body(osb_ref, ofx_ref, zsb_ref, zfx_ref, gsb_ref, gfx_ref, bm_ref, wsb_ref, wfx_ref, out_ref):
    def branch(o_ref, z_ref, w_ref):
        z = z_ref[...]
        a = o_ref[...] * (z * jax.nn.sigmoid(z))
        return _dot(a.astype(BF16), w_ref[...])

    bm = bm_ref[...]
    merged = (jax.nn.sigmoid(gsb_ref[...] + bm[0:1, :]) * branch(osb_ref, zsb_ref, wsb_ref)
              + jax.nn.sigmoid(gfx_ref[...] + bm[1:2, :]) * branch(ofx_ref, zfx_ref, wfx_ref))
    out_ref[...] = merged.astype(BF16)


def _merge(o_sb, o_fx, rest32, b_merge, w_br_sb16, w_br_fox16, d_model):
    m = o_sb.shape[0]
    tm = _pick_tile(m, 512)
    tn = _pick_tile(math.gcd(d_model, HEAD_W), 1024)
    g_off = 2 * HEAD_W // tn
    nd = d_model // tn
    row_spec = pl.BlockSpec((tm, HEAD_W), lambda i, j: (i, 0))
    return pl.pallas_call(
        _merge_body,
        grid=(m // tm, nd),
        in_specs=[row_spec, row_spec,
                  pl.BlockSpec((tm, HEAD_W), lambda i, j: (i, 0)),
                  pl.BlockSpec((tm, HEAD_W), lambda i, j: (i, 1)),
                  pl.BlockSpec((tm, tn), lambda i, j: (i, g_off + j)),
                  pl.BlockSpec((tm, tn), lambda i, j: (i, g_off + nd + j)),
                  pl.BlockSpec((2, tn), lambda i, j: (0, j)),
                  pl.BlockSpec((HEAD_W, tn), lambda i, j: (0, j)),
                  pl.BlockSpec((HEAD_W, tn), lambda i, j: (0, j))],
        out_specs=pl.BlockSpec((tm, tn), lambda i, j: (i, j)),
        out_shape=jax.ShapeDtypeStruct((m, d_model), BF16),
        compiler_params=_params(("parallel", "arbitrary")),
        name="merge",
    )(o_sb, o_fx, rest32, rest32, rest32, rest32, b_merge, w_br_sb16, w_br_fox16)


def _mix_ln_body(x_ref, gi_ref, bi_ref, mg_ref, wo_ref, g_ref, b_ref, out_ref, *, alpha):
    h = _layer_norm(x_ref[...], gi_ref[...], bi_ref[...])
    y = _dot(mg_ref[...], wo_ref[...])
    out_ref[...] = _layer_norm(alpha * h + y, g_ref[...], b_ref[...])


def _mix_ln(x, ln_in_g, ln_in_b, merged16, w_o16, g, b, alpha):
    m, d = x.shape
    tm = _pick_tile(m, 256)
    row = pl.BlockSpec((tm, d), lambda i: (i, 0))
    vec = pl.BlockSpec((1, d), lambda i: (0, 0))
    return pl.pallas_call(
        functools.partial(_mix_ln_body, alpha=alpha),
        grid=(m // tm,),
        in_specs=[row, vec, vec, row, pl.BlockSpec((d, d), lambda i: (0, 0)), vec, vec],
        out_specs=row,
        out_shape=jax.ShapeDtypeStruct((m, d), F32),
        compiler_params=_params(("parallel",)),
        name="mix_ln",
    )(x, ln_in_g.reshape(1, d), ln_in_b.reshape(1, d), merged16, w_o16, g.reshape(1, d), b.reshape(1, d))


def _ple_body(h_ref, p_ref, wpg_ref, bpg_ref, wpe_ref, g_ref, b_ref, out_ref, *, alpha):
    h = h_ref[...]
    gate = jax.nn.sigmoid(_dot(h.astype(BF16), wpg_ref[...]) + bpg_ref[...])
    e = _dot(p_ref[...].astype(BF16), wpe_ref[...])
    out_ref[...] = _layer_norm(alpha * h + gate * e, g_ref[...], b_ref[...])


def _ple(h, p, w_pg16, b_pg, w_pe16, g, b, alpha):
    m, d = h.shape
    dp = p.shape[1]
    tm = _pick_tile(m, 256)
    row = pl.BlockSpec((tm, d), lambda i: (i, 0))
    vec = pl.BlockSpec((1, d), lambda i: (0, 0))
    return pl.pallas_call(
        functools.partial(_ple_body, alpha=alpha),
        grid=(m // tm,),
        in_specs=[row, pl.BlockSpec((tm, dp), lambda i: (i, 0)),
                  pl.BlockSpec((d, d), lambda i: (0, 0)), vec,
                  pl.BlockSpec((dp, d), lambda i: (0, 0)), vec, vec],
        out_specs=row,
        out_shape=jax.ShapeDtypeStruct((m, d), F32),
        compiler_params=_params(("parallel",)),
        name="ple",
    )(h, p, w_pg16, b_pg.reshape(1, d), w_pe16, g.reshape(1, d), b.reshape(1, d))


def kernel(x_prompt, x_sample, cache_sb_k, cache_sb_v, cache_fox_k, cache_fox_v, cache_fox_logf, page_table, p_prompt, p_sample, ln_in_g, ln_in_b, w_in, b_f, b_merge, w_br_sb, w_br_fox, w_o, ln_mix_g, ln_mix_b, w_pe, w_pg, b_pg, ln_ple_g, ln_ple_b):
    depth = w_in.shape[0]
    assert depth == 1, "single-layer trunk only"
    batch, seq, d_model = x_prompt.shape
    n_seq, n_new, _ = x_sample.shape
    alpha = (2.0 * depth) ** 0.25
    assert w_in.shape[2] == 8 * HEAD_W + N_HEADS + 2 * d_model

    w = w_in[0]
    qkv_cols = jnp.concatenate([w[:, 0:3 * HEAD_W], w[:, 4 * HEAD_W:7 * HEAD_W]], axis=1).astype(BF16)
    rest_cols = jnp.concatenate([w[:, 3 * HEAD_W:4 * HEAD_W], w[:, 7 * HEAD_W:8 * HEAD_W],
                                 w[:, 8 * HEAD_W + N_HEADS:]], axis=1).astype(BF16)
    wf16 = jnp.pad(w[:, 8 * HEAD_W:8 * HEAD_W + N_HEADS], ((0, 0), (0, HEAD_DIM - N_HEADS))).astype(BF16)
    bf_pad = jnp.pad(b_f[0], (0, HEAD_DIM - N_HEADS)).reshape(1, HEAD_DIM)
    w_br_sb16 = w_br_sb[0].astype(BF16)
    w_br_fox16 = w_br_fox[0].astype(BF16)
    w_o16 = w_o[0].astype(BF16)
    w_pg16 = w_pg[0].astype(BF16)
    w_pe16 = w_pe[0].astype(BF16)

    def project(x2d):
        qkv32, qkv16 = _ln_proj(x2d, ln_in_g, ln_in_b, qkv_cols, emit16=True)
        rest32, logf = _ln_proj(x2d, ln_in_g, ln_in_b, rest_cols, emit16=False, wf16=wf16, bf=bf_pad)
        return qkv32, qkv16, rest32, logf

    def finish(x2d, o_sb, o_fx, rest32, p2d):
        merged16 = _merge(o_sb, o_fx, rest32, b_merge[0], w_br_sb16, w_br_fox16, d_model)
        h = _mix_ln(x2d, ln_in_g, ln_in_b, merged16, w_o16, ln_mix_g[0], ln_mix_b[0], alpha)
        return _ple(h, p2d, w_pg16, b_pg[0], w_pe16, ln_ple_g[0], ln_ple_b[0], alpha)

    def new_rows(qkv32, logf, lead):
        kv = [qkv32[:, c * HEAD_W:(c + 1) * HEAD_W].reshape(1, *lead, N_HEADS, HEAD_DIM) for c in (1, 2, 4, 5)]
        return kv + [logf[:, :N_HEADS].reshape(1, *lead, N_HEADS)]

    xp = x_prompt.reshape(batch * seq, d_model)
    qkv32_p, qkv16_p, rest32_p, logf_p = project(xp)
    fq, frow = _fcum(logf_p, batch, seq)
    o_sb_p, o_fx_p = _prompt_attention(qkv16_p, fq, frow, batch, seq)
    y_prompt = finish(xp, o_sb_p, o_fx_p, rest32_p, p_prompt[0].reshape(batch * seq, -1))

    xs = x_sample.reshape(n_seq * n_new, d_model)
    qkv32_s, _, rest32_s, logf_s = project(xs)

    def heads_major(a, c):
        return jnp.swapaxes(a[:, c * HEAD_W:(c + 1) * HEAD_W].reshape(n_seq, n_new, N_HEADS, HEAD_DIM), 1, 2)

    def query_rows(a, c):
        q = jnp.pad(heads_major(a, c), ((0, 0), (0, 0), (0, Q_PAD - n_new), (0, 0)))
        return q.reshape(n_seq, N_HEADS * Q_PAD, HEAD_DIM)

    def token_rows(o):
        o = o.reshape(n_seq, N_HEADS, Q_PAD, HEAD_DIM)[:, :, :n_new]
        return jnp.swapaxes(o, 1, 2).reshape(n_seq * n_new, HEAD_W)

    def key_head_rows(cache):
        n_pool, page = cache.shape[1:3]
        return cache[0].reshape(n_pool, page * N_HEADS, HEAD_DIM)

    lf_new = logf_s[:, :N_HEADS].reshape(n_seq, n_new, N_HEADS)
    lf_row = jnp.pad(jnp.swapaxes(lf_new, 1, 2), ((0, 0), (0, 0), (0, HEAD_DIM - n_new)))
    o_sb_s, o_fx_s = _sample_attention(
        page_table, query_rows(qkv32_s, 0), query_rows(qkv32_s, 3),
        heads_major(qkv32_s, 1), heads_major(qkv32_s, 2), heads_major(qkv32_s, 4), heads_major(qkv32_s, 5),
        lf_row, key_head_rows(cache_sb_k), key_head_rows(cache_sb_v), key_head_rows(cache_fox_k),
        key_head_rows(cache_fox_v), jnp.swapaxes(cache_fox_logf[0], 1, 2), pages_per_step=4)
    y_sample = finish(xs, token_rows(o_sb_s), token_rows(o_fx_s), rest32_s, p_sample[0].reshape(n_seq * n_new, -1))

    return (y_prompt.reshape(batch, seq, d_model), y_sample.reshape(n_seq, n_new, d_model),
            *new_rows(qkv32_p, logf_p, (batch, seq)), *new_rows(qkv32_s, logf_s, (n_seq, n_new)))
```

```python
import functools
import math

import jax
import jax.numpy as jnp
from jax import lax
from jax.experimental import pallas as pl
from jax.experimental.pallas import tpu as pltpu

F32 = jnp.float32
BF16 = jnp.bfloat16

HEAD_DIM = 128
N_HEADS = 8
HEAD_W = N_HEADS * HEAD_DIM
LN_EPS = 1e-5
QK_SCALE = 1.0 / math.sqrt(HEAD_DIM)
NEG_BIG = -1e30
EXP_ZERO_BELOW = -105.0
NORM_SLACK = 1.01
V7X_VMEM_LIMIT = 56 * 1024 * 1024

NT_DIMS = (((1,), (1,)), ((), ()))


def _nt_dot(a, b):
    return lax.dot_general(a, b, NT_DIMS, preferred_element_type=F32)


def _dot(a, b):
    return jnp.dot(a, b, preferred_element_type=F32)


def _log_sigmoid(x):
    return jnp.minimum(x, 0.0) - jnp.log1p(jnp.exp(-jnp.abs(x)))


def _split2(x):
    hi = x.astype(BF16)
    lo = (x - hi.astype(F32)).astype(BF16)
    return hi, lo


def _split3(x):
    hi = x.astype(BF16)
    r = x - hi.astype(F32)
    mid = r.astype(BF16)
    lo = (r - mid.astype(F32)).astype(BF16)
    return hi, mid, lo


def _layer_norm(x, g, b):
    mu = jnp.mean(x, axis=-1, keepdims=True)
    xc = x - mu
    var = jnp.mean(xc * xc, axis=-1, keepdims=True)
    return xc * lax.rsqrt(var + LN_EPS) * g + b


def _pick_tile(n, limit):
    if n <= HEAD_DIM:
        return n
    best = None
    for t in range(HEAD_DIM, min(n, limit) + 1, HEAD_DIM):
        if n % t == 0:
            best = t
    assert best is not None, (n, limit)
    return best


def _params(sem):
    return pltpu.CompilerParams(dimension_semantics=sem, vmem_limit_bytes=V7X_VMEM_LIMIT)


def _upper_strict(n):
    row = lax.broadcasted_iota(jnp.int32, (n, n), 0)
    col = lax.broadcasted_iota(jnp.int32, (n, n), 1)
    return jnp.where(row > col, 1.0, 0.0).astype(BF16)


def _rev_excl_cumsum(x, tri):
    rows = x.shape[0]
    hi, lo = _split2(x)
    both = _dot(jnp.concatenate([hi, lo], axis=0), tri)
    return both[:rows] + both[rows:]


def _ln_proj_body(*refs, emit16, aux):
    x_ref, g_ref, b_ref, w_ref = refs[:4]
    pos = 4
    if aux:
        wf_ref, bf_ref = refs[pos:pos + 2]
        pos += 2
    y32_ref = refs[pos]
    pos += 1
    if emit16:
        y16_ref = refs[pos]
        pos += 1
    if aux:
        logf_ref = refs[pos]
        pos += 1
    xs_ref = refs[pos]

    @pl.when(pl.program_id(1) == 0)
    def _():
        h = _layer_norm(x_ref[...], g_ref[...], b_ref[...])
        xs_ref[...] = h.astype(BF16)
        if aux:
            f = _dot(xs_ref[...], wf_ref[...]) + bf_ref[...]
            logf_ref[...] = _log_sigmoid(f)

    y = _dot(xs_ref[...], w_ref[...])
    y32_ref[...] = y
    if emit16:
        y16_ref[...] = y.astype(BF16)


def _ln_proj(x, g, b, w16, *, emit16, wf16=None, bf=None):
    m, d = x.shape
    n = w16.shape[1]
    tm = _pick_tile(m, 512)
    tn = _pick_tile(n, 1024)
    aux = wf16 is not None
    in_specs = [
        pl.BlockSpec((tm, d), lambda i, j: (i, 0)),
        pl.BlockSpec((1, d), lambda i, j: (0, 0)),
        pl.BlockSpec((1, d), lambda i, j: (0, 0)),
        pl.BlockSpec((d, tn), lambda i, j: (0, j)),
    ]
    args = [x, g.reshape(1, d), b.reshape(1, d), w16]
    out_shape = [jax.ShapeDtypeStruct((m, n), F32)]
    out_specs = [pl.BlockSpec((tm, tn), lambda i, j: (i, j))]
    if emit16:
        out_shape.append(jax.ShapeDtypeStruct((m, n), BF16))
        out_specs.append(pl.BlockSpec((tm, tn), lambda i, j: (i, j)))
    if aux:
        in_specs += [pl.BlockSpec((d, HEAD_DIM), lambda i, j: (0, 0)),
                     pl.BlockSpec((1, HEAD_DIM), lambda i, j: (0, 0))]
        args += [wf16, bf]
        out_shape.append(jax.ShapeDtypeStruct((m, HEAD_DIM), F32))
        out_specs.append(pl.BlockSpec((tm, HEAD_DIM), lambda i, j: (i, 0)))
    return pl.pallas_call(
        functools.partial(_ln_proj_body, emit16=emit16, aux=aux),
        grid=(m // tm, n // tn),
        in_specs=in_specs,
        out_specs=out_specs,
        out_shape=out_shape,
        scratch_shapes=[pltpu.VMEM((tm, d), BF16)],
        compiler_params=_params(("parallel", "arbitrary")),
        name="ln_proj",
    )(*args)


def _fcum_body(lf_ref, fq_ref, frow_ref, carry_ref, *, tc):
    @pl.when(pl.program_id(1) == 0)
    def _():
        carry_ref[...] = jnp.zeros_like(carry_ref)

    row = lax.broadcasted_iota(jnp.int32, (tc, tc), 0)
    col = lax.broadcasted_iota(jnp.int32, (tc, tc), 1)
    lower_incl = jnp.where(col <= row, 1.0, 0.0).astype(BF16)
    hi, mid, lo = _split3(lf_ref[...])
    f = _dot(lower_incl, hi) + _dot(lower_incl, mid) + _dot(lower_incl, lo) + carry_ref[...]
    fq_ref[...] = f
    carry_ref[...] = f[tc - 1:tc, :]
    sel = jnp.where(lax.broadcasted_iota(jnp.int32, (N_HEADS, HEAD_DIM), 0)
                    == lax.broadcasted_iota(jnp.int32, (N_HEADS, HEAD_DIM), 1), 1.0, 0.0).astype(BF16)
    fh, fm, fl = _split3(f)
    frow_ref[...] = _nt_dot(sel, fh) + _nt_dot(sel, fm) + _nt_dot(sel, fl)


def _fcum(logf, batch, seq):
    tc = min(256, seq)
    nc = seq // tc
    return pl.pallas_call(
        functools.partial(_fcum_body, tc=tc),
        grid=(batch, nc),
        in_specs=[pl.BlockSpec((tc, HEAD_DIM), lambda b, c: (b * nc + c, 0))],
        out_specs=[pl.BlockSpec((tc, HEAD_DIM), lambda b, c: (b * nc + c, 0)),
                   pl.BlockSpec((None, N_HEADS, tc), lambda b, c: (b, 0, c))],
        out_shape=[jax.ShapeDtypeStruct((batch * seq, HEAD_DIM), F32),
                   jax.ShapeDtypeStruct((batch, N_HEADS, seq), F32)],
        scratch_shapes=[pltpu.VMEM((1, HEAD_DIM), F32)],
        compiler_params=_params(("parallel", "arbitrary")),
        name="fcum",
    )(logf)


def _sb_prompt_body(q_ref, k_ref, v_ref, o_ref, *, tq):
    i = pl.program_id(2)
    q = q_ref[...]
    tri = _upper_strict(tq)
    row = lax.broadcasted_iota(jnp.int32, (tq, tq), 0)
    col = lax.broadcasted_iota(jnp.int32, (tq, tq), 1)
    causal = col < row

    def block(j, carry, masked):
        acc, c = carry
        start = pl.multiple_of(j * tq, tq)
        k = k_ref[pl.ds(start, tq), :]
        v = v_ref[pl.ds(start, tq), :]
        z = _nt_dot(q, k) * QK_SCALE
        lsn = _log_sigmoid(-z)
        lk = jnp.where(causal, lsn, 0.0) if masked else lsn
        later = _rev_excl_cumsum(lk, tri) + c
        a = jnp.exp(z + lsn + later)
        if masked:
            a = jnp.where(causal, a, 0.0)
        acc = acc + _dot(a.astype(BF16), v)
        c = c + jnp.sum(lk, axis=-1, keepdims=True)
        return acc, c

    def live(c):
        return jnp.max(c) > EXP_ZERO_BELOW

    def cond(state):
        jj, _, _, alive = state
        return jnp.logical_and(jj < i, alive)

    def body(state):
        jj, acc, c, _ = state
        acc, c = block(i - 1 - jj, (acc, c), False)
        return jj + 1, acc, c, live(c)

    acc, c = block(i, (jnp.zeros((tq, HEAD_DIM), F32), jnp.zeros((tq, 1), F32)), True)
    _, acc, _, _ = lax.while_loop(cond, body, (jnp.int32(0), acc, c, live(c)))
    o_ref[...] = acc


FOX_CHUNK_BLOCKS = 4


def _fox_prompt_body(q_ref, k_ref, v_ref, fq_ref, frow_ref, o_ref, acc_ref, m_ref, l_ref, kmax_ref, *, tq):
    h = pl.program_id(1)
    i = pl.program_id(2)

    @pl.when(i == 0)
    def _():
        k = k_ref[...].astype(F32)
        kmax_ref[...] = jnp.sqrt(jnp.max(jnp.sum(k * k, axis=-1, keepdims=True), axis=0, keepdims=True))

    q = q_ref[...]
    lane = lax.broadcasted_iota(jnp.int32, (tq, HEAD_DIM), 1)
    fq = jnp.sum(jnp.where(lane == h, fq_ref[...], 0.0), axis=-1, keepdims=True)
    qf = q.astype(F32)
    reach = jnp.sqrt(jnp.sum(qf * qf, axis=-1, keepdims=True)) * kmax_ref[...] * (QK_SCALE * NORM_SLACK) + fq

    def chunk(jb, nb, masked):
        start = pl.multiple_of(jb * tq, tq)
        k = k_ref[pl.ds(start, nb * tq), :]
        v = v_ref[pl.ds(start, nb * tq), :]
        fk = jnp.concatenate([frow_ref[h, pl.ds(jb + t, 1), :] for t in range(nb)], axis=1)
        logits = _nt_dot(q, k) * QK_SCALE + (fq - fk)
        if masked:
            row = lax.broadcasted_iota(jnp.int32, (tq, tq), 0)
            col = lax.broadcasted_iota(jnp.int32, (tq, tq), 1)
            logits = jnp.where(col <= row, logits, NEG_BIG)
        m_old = m_ref[...]
        m_new = jnp.maximum(m_old, jnp.max(logits, axis=-1, keepdims=True))
        alpha = jnp.exp(m_old - m_new)
        p = jnp.exp(logits - m_new)
        m_ref[...] = m_new
        l_ref[...] = l_ref[...] * alpha + jnp.sum(p, axis=-1, keepdims=True)
        acc_ref[...] = acc_ref[...] * alpha + _dot(p.astype(BF16), v)

    acc_ref[...] = jnp.zeros_like(acc_ref)
    m_ref[...] = jnp.full(m_ref.shape, NEG_BIG, F32)
    l_ref[...] = jnp.zeros_like(l_ref)
    chunk(i, 1, True)

    left = i
    for nb in (1, 2):
        if nb < FOX_CHUNK_BLOCKS:
            take = jnp.bitwise_and(left, nb)
            pl.when(take != 0)(functools.partial(chunk, left - nb, nb, False))
            left = left - take

    def live(n_left):
        newest = frow_ref[h, pl.ds(jnp.maximum(n_left - 1, 0), 1), :][:, tq - 1:tq]
        return jnp.max(reach - newest - m_ref[...]) > EXP_ZERO_BELOW

    def cond(state):
        n_left, alive = state
        return jnp.logical_and(n_left > 0, alive)

    def body(state):
        n_left, _ = state
        n_left = n_left - FOX_CHUNK_BLOCKS
        chunk(n_left, FOX_CHUNK_BLOCKS, False)
        return n_left, live(n_left)

    lax.while_loop(cond, body, (left, live(left)))
    o_ref[...] = acc_ref[...] / l_ref[...]


def _prompt_attention(qkv16, fq, frow, batch, seq):
    tq = min(256, seq)
    nq = seq // tq
    m = batch * seq
    grid = (batch, N_HEADS, nq)

    def q_spec(group):
        return pl.BlockSpec((tq, HEAD_DIM), lambda b, h, i: (b * nq + i, group * N_HEADS + h))

    def kv_spec(group):
        return pl.BlockSpec((seq, HEAD_DIM), lambda b, h, i: (b, group * N_HEADS + h))

    o_spec = pl.BlockSpec((tq, HEAD_DIM), lambda b, h, i: (b * nq + i, h))
    o_shape = jax.ShapeDtypeStruct((m, HEAD_W), F32)
    sem = ("parallel", "parallel", "arbitrary")
    o_sb = pl.pallas_call(
        functools.partial(_sb_prompt_body, tq=tq),
        grid=grid,
        in_specs=[q_spec(0), kv_spec(1), kv_spec(2)],
        out_specs=o_spec, out_shape=o_shape,
        compiler_params=_params(sem), name="sb_prompt",
    )(qkv16, qkv16, qkv16)
    o_fx = pl.pallas_call(
        functools.partial(_fox_prompt_body, tq=tq),
        grid=grid,
        in_specs=[q_spec(3), kv_spec(4), kv_spec(5),
                  pl.BlockSpec((tq, HEAD_DIM), lambda b, h, i: (b * nq + i, 0)),
                  pl.BlockSpec((None, N_HEADS, nq, tq), lambda b, h, i: (b, 0, 0, 0))],
        out_specs=o_spec, out_shape=o_shape,
        scratch_shapes=[pltpu.VMEM((tq, HEAD_DIM), F32), pltpu.VMEM((tq, 1), F32), pltpu.VMEM((tq, 1), F32),
                        pltpu.VMEM((1, 1), F32)],
        compiler_params=_params(sem), name="fox_prompt",
    )(qkv16, qkv16, qkv16, fq, frow.reshape(batch, N_HEADS, nq, tq))
    return o_sb, o_fx


Q_PAD = 8
N_PAIRS = N_HEADS // 2
PAIR_W = 2 * HEAD_DIM


def _expand_heads(x):
    n = x.shape[1]
    return jnp.concatenate([jnp.broadcast_to(x[h:h + 1, :], (Q_PAD, n)) for h in range(N_HEADS)], axis=0)


SB_LIVE = 0
SB_FETCHED = 1
FV_STARTED = 3
N_FLAGS = 8


def _sample_body(pt_ref, qs_ref, qf_ref, ksn_ref, vsn_ref, kfn_ref, vfn_ref, lfr_ref, tri_ref,
                 csk_hbm, csv_hbm, cfk_hbm, cfv_hbm, clf_hbm, osb_ref, ofx_ref,
                 qbs_ref, qbf_ref, accs_ref, accf_ref, cs_ref, m_ref, l_ref, cg_ref, cq_ref, pad_ref,
                 skb, svb, fkb, fvb, lfb, pr_ref, alpha_ref, flags,
                 sk_sem, sv_sem, fk_sem, fv_sem, lf_sem, *, n_new, pages_per_step, page, chunk):
    g = pages_per_step
    rows = N_HEADS * Q_PAD
    s = pl.program_id(0)
    p = pl.program_id(1)
    n_seq = pl.num_programs(0)
    n_steps = pl.num_programs(1)
    cur = lax.rem(s * n_steps + p, 2)
    nxt = 1 - cur
    last = p == n_steps - 1

    def copies(hbm, buf, sem, seq, step, slot):
        return [pltpu.make_async_copy(hbm.at[pt_ref[seq, (n_steps - 1 - step) * g + gi]],
                                      buf.at[slot, gi], sem.at[slot]) for gi in range(g)]

    def start(*stream):
        for cp in copies(*stream):
            cp.start()

    def wait(*stream):
        for cp in copies(*stream):
            cp.wait()

    sk, sv, fk, fv, lf_ = ((csk_hbm, skb, sk_sem), (csv_hbm, svb, sv_sem), (cfk_hbm, fkb, fk_sem),
                           (cfv_hbm, fvb, fv_sem), (clf_hbm, lfb, lf_sem))

    @pl.when(jnp.logical_and(s == 0, p == 0))
    def _cold_start():
        for stream in (fk, lf_, sk, sv):
            start(*stream, 0, 0, 0)
        for i in range(N_FLAGS):
            flags[i] = 0
        flags[SB_FETCHED] = 1

    s_next = jnp.where(last, s + 1, s)
    p_next = jnp.where(last, 0, p + 1)

    @pl.when(s_next < n_seq)
    def _prefetch():
        start(*fk, s_next, p_next, nxt)
        start(*lf_, s_next, p_next, nxt)
        fetch_sb = jnp.logical_or(jnp.logical_or(last, p == 0), flags[SB_LIVE] != 0)
        flags[SB_FETCHED + nxt] = fetch_sb.astype(jnp.int32)

        @pl.when(fetch_sb)
        def _():
            start(*sk, s_next, p_next, nxt)
            start(*sv, s_next, p_next, nxt)

    def pair_rows(x, pp):
        return x[pp * 2 * Q_PAD:(pp + 1) * 2 * Q_PAD]

    def scores(qb_ref, key_pair):
        return jnp.concatenate(
            [_nt_dot(qb_ref[pp], key_pair(pp)) for pp in range(N_PAIRS)], axis=0) * QK_SCALE

    def cache_pair(buf, slot, page_slots):
        def head_rows(gi, h):
            return buf[slot, gi, pl.ds(h, page, stride=N_HEADS), :]

        def build(pp):
            return jnp.concatenate(
                [jnp.concatenate([head_rows(gi, 2 * pp), head_rows(gi, 2 * pp + 1)], axis=1)
                 for gi in page_slots], axis=0).astype(BF16)
        return build

    @pl.when(p == 0)
    def _init():
        zeros = jnp.zeros((Q_PAD, HEAD_DIM), F32)

        def block_diag(q_ref_, qb_ref):
            for pp in range(N_PAIRS):
                top = jnp.concatenate([q_ref_[(2 * pp) * Q_PAD:(2 * pp + 1) * Q_PAD, :], zeros], axis=1)
                bot = jnp.concatenate([zeros, q_ref_[(2 * pp + 1) * Q_PAD:(2 * pp + 2) * Q_PAD, :]], axis=1)
                qb_ref[pp] = jnp.concatenate([top, bot], axis=0).astype(BF16)

        block_diag(qs_ref, qbs_ref)
        block_diag(qf_ref, qbf_ref)

        kw = pad_ref.shape[0]
        qi = lax.broadcasted_iota(jnp.int32, (rows, kw), 0) & (Q_PAD - 1)
        ki = lax.broadcasted_iota(jnp.int32, (rows, kw), 1)
        tri = tri_ref[0:kw, 0:kw]

        def new_pair(ref):
            def build(pp):
                pad_ref[...] = jnp.zeros_like(pad_ref)
                pad_ref[0:n_new, 0:HEAD_DIM] = ref[2 * pp]
                pad_ref[0:n_new, HEAD_DIM:PAIR_W] = ref[2 * pp + 1]
                return pad_ref[...].astype(BF16)
            return build

        mask = ki < qi
        z = scores(qbs_ref, new_pair(ksn_ref))
        lsn = _log_sigmoid(-z)
        lk = jnp.where(mask, lsn, 0.0)
        later = _rev_excl_cumsum(lk, tri)
        a = jnp.where(mask, jnp.exp(z + lsn + later), 0.0).astype(BF16)
        v_pair = new_pair(vsn_ref)
        for pp in range(N_PAIRS):
            accs_ref[pp] = _dot(pair_rows(a, pp), v_pair(pp))
        cs_ref[...] = jnp.sum(lk, axis=-1, keepdims=True)

        lfr = lfr_ref[...]
        ck = jnp.sum(lfr, axis=-1, keepdims=True) - _rev_excl_cumsum(lfr, tri)
        ck = _expand_heads(ck)
        cq = jnp.sum(jnp.where(ki == qi, ck, 0.0), axis=-1, keepdims=True)
        cq_ref[...] = cq
        logits = scores(qbf_ref, new_pair(kfn_ref)) + (cq - ck)
        logits = jnp.where(ki <= qi, logits, NEG_BIG)
        m0 = jnp.max(logits, axis=-1, keepdims=True)
        pr = jnp.exp(logits - m0)
        m_ref[...] = m0
        l_ref[...] = jnp.sum(pr, axis=-1, keepdims=True)
        pr = pr.astype(BF16)
        v_pair = new_pair(vfn_ref)
        for pp in range(N_PAIRS):
            accf_ref[pp] = _dot(pair_rows(pr, pp), v_pair(pp))
        cg_ref[...] = jnp.zeros_like(cg_ref)
        flags[SB_LIVE] = 1

    tri = tri_ref[...]
    pages_per_chunk = chunk // page
    n_chunks = (g * page) // chunk
    all_pages = range(g)

    def chunk_pages(ci):
        return range(ci * pages_per_chunk, (ci + 1) * pages_per_chunk)

    wait(*fk, s, p, cur)
    wait(*lf_, s, p, cur)
    parts = [None] * n_chunks
    for ci in reversed(range(n_chunks)):
        lf = jnp.concatenate([lfb[cur, gi] for gi in chunk_pages(ci)], axis=1)
        gl = _rev_excl_cumsum(lf, tri) + cg_ref[...]
        parts[ci] = scores(qbf_ref, cache_pair(fkb, cur, chunk_pages(ci))) + (_expand_heads(gl) + cq_ref[...])
        cg_ref[...] += jnp.sum(lf, axis=-1, keepdims=True)
    logits = jnp.concatenate(parts, axis=1)
    m_old = m_ref[...]
    row_max = jnp.max(logits, axis=-1, keepdims=True)
    m_new = jnp.maximum(m_old, row_max)
    alpha = jnp.exp(m_old - m_new)
    pr = jnp.exp(logits - m_new)
    m_ref[...] = m_new
    l_ref[...] = l_ref[...] * alpha + jnp.sum(pr, axis=-1, keepdims=True)
    need_values = jnp.max(row_max - m_new) > EXP_ZERO_BELOW

    def apply_values(slot):
        a_slot = alpha_ref[slot]
        pr_slot = pr_ref[slot]
        v_pair = cache_pair(fvb, slot, all_pages)
        for pp in range(N_PAIRS):
            accf_ref[pp] = accf_ref[pp] * pair_rows(a_slot, pp) + _dot(pair_rows(pr_slot, pp), v_pair(pp))

    @pl.when(jnp.logical_and(p > 0, flags[FV_STARTED + nxt] != 0))
    def _previous_group_values():
        wait(*fv, s, p - 1, nxt)
        apply_values(nxt)
        flags[FV_STARTED + nxt] = 0

    @pl.when(need_values)
    def _request_values():
        pr_ref[cur] = pr.astype(BF16)
        alpha_ref[cur] = alpha
        start(*fv, s, p, cur)
        flags[FV_STARTED + cur] = 1

    @pl.when(flags[SB_FETCHED + cur] != 0)
    def _():
        wait(*sk, s, p, cur)
        wait(*sv, s, p, cur)

    @pl.when(jnp.logical_and(flags[SB_FETCHED + cur] != 0, flags[SB_LIVE] != 0))
    def _stick_breaking():
        for ci in reversed(range(n_chunks)):
            z = scores(qbs_ref, cache_pair(skb, cur, chunk_pages(ci)))
            lsn = _log_sigmoid(-z)
            later = _rev_excl_cumsum(lsn, tri) + cs_ref[...]
            a = jnp.exp(z + lsn + later).astype(BF16)
            v_pair = cache_pair(svb, cur, chunk_pages(ci))
            for pp in range(N_PAIRS):
                accs_ref[pp] += _dot(pair_rows(a, pp), v_pair(pp))
            cs_ref[...] += jnp.sum(lsn, axis=-1, keepdims=True)
        real_row = (lax.broadcasted_iota(jnp.int32, (rows, 1), 0) & (Q_PAD - 1)) < n_new
        flags[SB_LIVE] = (jnp.max(jnp.where(real_row, cs_ref[...], NEG_BIG)) > EXP_ZERO_BELOW).astype(jnp.int32)

    @pl.when(last)
    def _fin():
        @pl.when(flags[FV_STARTED + cur] != 0)
        def _():
            wait(*fv, s, p, cur)
            apply_values(cur)
            flags[FV_STARTED + cur] = 0

        inv_l = 1.0 / l_ref[...]
        for h in range(N_HEADS):
            pp, odd = divmod(h, 2)
            r0, c0 = odd * Q_PAD, odd * HEAD_DIM
            osb_ref[h * Q_PAD:(h + 1) * Q_PAD, :] = accs_ref[pp, r0:r0 + Q_PAD, c0:c0 + HEAD_DIM]
            ofx_ref[h * Q_PAD:(h + 1) * Q_PAD, :] = (accf_ref[pp, r0:r0 + Q_PAD, c0:c0 + HEAD_DIM]
                                                     * inv_l[h * Q_PAD:(h + 1) * Q_PAD, :])


def _sample_attention(page_table, q_sb, q_fx, k_sb, v_sb, k_fx, v_fx, lf_row,
                      c_sb_k, c_sb_v, c_fx_k, c_fx_v, c_lf_t, *, pages_per_step):
    n_seq, _, n_new, _ = k_sb.shape
    n_pages = page_table.shape[1]
    page = c_sb_k.shape[1] // N_HEADS
    g = pages_per_step
    assert n_pages % g == 0 and n_new <= Q_PAD
    n_steps = n_pages // g
    rows = N_HEADS * Q_PAD
    chunk = min(2 * page, g * page)
    assert (g * page) % chunk == 0
    tri = _upper_strict(chunk)

    def seq_spec(shape):
        return pl.BlockSpec((None,) + shape, lambda s, p, pt: (s,) + (0,) * len(shape))

    in_specs = ([seq_spec((rows, HEAD_DIM))] * 2 + [seq_spec((N_HEADS, n_new, HEAD_DIM))] * 4
                + [seq_spec((N_HEADS, HEAD_DIM)), pl.BlockSpec((chunk, chunk), lambda s, p, pt: (0, 0))]
                + [pl.BlockSpec(memory_space=pl.ANY)] * 5)
    args = [q_sb, q_fx, k_sb, v_sb, k_fx, v_fx, lf_row, tri, c_sb_k, c_sb_v, c_fx_k, c_fx_v, c_lf_t]
    page_buf = pltpu.VMEM((2, g, page * N_HEADS, HEAD_DIM), F32)
    dma_sem = pltpu.SemaphoreType.DMA((2,))

    grid_spec = pltpu.PrefetchScalarGridSpec(
        num_scalar_prefetch=1,
        grid=(n_seq, n_steps),
        in_specs=in_specs,
        out_specs=[seq_spec((rows, HEAD_DIM))] * 2,
        scratch_shapes=[
            pltpu.VMEM((N_PAIRS, 2 * Q_PAD, PAIR_W), BF16), pltpu.VMEM((N_PAIRS, 2 * Q_PAD, PAIR_W), BF16),
            pltpu.VMEM((N_PAIRS, 2 * Q_PAD, PAIR_W), F32), pltpu.VMEM((N_PAIRS, 2 * Q_PAD, PAIR_W), F32),
            pltpu.VMEM((rows, 1), F32), pltpu.VMEM((rows, 1), F32), pltpu.VMEM((rows, 1), F32),
            pltpu.VMEM((N_HEADS, 1), F32), pltpu.VMEM((rows, 1), F32),
            pltpu.VMEM((HEAD_DIM, PAIR_W), F32),
            page_buf, page_buf, page_buf, page_buf, pltpu.VMEM((2, g, N_HEADS, page), F32),
            pltpu.VMEM((2, rows, g * page), BF16), pltpu.VMEM((2, rows, 1), F32),
            pltpu.SMEM((N_FLAGS,), jnp.int32),
            dma_sem, dma_sem, dma_sem, dma_sem, dma_sem,
        ],
    )
    return pl.pallas_call(
        functools.partial(_sample_body, n_new=n_new, pages_per_step=g, page=page, chunk=chunk),
        grid_spec=grid_spec,
        out_shape=[jax.ShapeDtypeStruct((n_seq, rows, HEAD_DIM), F32)] * 2,
        compiler_params=_params(("arbitrary", "arbitrary")),
        name="sample_attn",
    )(page_table, *args)


def _merge_body(osb_ref, ofx_ref, zsb_ref, zfx_ref, gsb_ref, gfx_ref, bm_ref, wsb_ref, wfx_ref, out_ref):
    def branch(o_ref, z_ref, w_ref):
        z = z_ref[...]
        a = o_ref[...] * (z * jax.nn.sigmoid(z))
        return _dot(a.astype(BF16), w_ref[...])

    bm = bm_ref[...]
    merged = (jax.nn.sigmoid(gsb_ref[...] + bm[0:1, :]) * branch(osb_ref, zsb_ref, wsb_ref)
              + jax.nn.sigmoid(gfx_ref[...] + bm[1:2, :]) * branch(ofx_ref, zfx_ref, wfx_ref))
    out_ref[...] = merged.astype(BF16)


def _merge(o_sb, o_fx, rest32, b_merge, w_br_sb16, w_br_fox16, d_model):
    m = o_sb.shape[0]
    tm = _pick_tile(m, 512)
    tn = _pick_tile(math.gcd(d_model, HEAD_W), 1024)
    g_off = 2 * HEAD_W // tn
    nd = d_model // tn
    row_spec = pl.BlockSpec((tm, HEAD_W), lambda i, j: (i, 0))
    return pl.pallas_call(
        _merge_body,
        grid=(m // tm, nd),
        in_specs=[row_spec, row_spec,
                  pl.BlockSpec((tm, HEAD_W), lambda i, j: (i, 0)),
                  pl.BlockSpec((tm, HEAD_W), lambda i, j: (i, 1)),
                  pl.BlockSpec((tm, tn), lambda i, j: (i, g_off + j)),
                  pl.BlockSpec((tm, tn), lambda i, j: (i, g_off + nd + j)),
                  pl.BlockSpec((2, tn), lambda i, j: (0, j)),
                  pl.BlockSpec((HEAD_W, tn), lambda i, j: (0, j)),
                  pl.BlockSpec((HEAD_W, tn), lambda i, j: (0, j))],
        out_specs=pl.BlockSpec((tm, tn), lambda i, j: (i, j)),
        out_shape=jax.ShapeDtypeStruct((m, d_model), BF16),
        compiler_params=_params(("parallel", "arbitrary")),
        name="merge",
    )(o_sb, o_fx, rest32, rest32, rest32, rest32, b_merge, w_br_sb16, w_br_fox16)


def _mix_ln_body(x_ref, gi_ref, bi_ref, mg_ref, wo_ref, g_ref, b_ref, out_ref, *, alpha):
    h = _layer_norm(x_ref[...], gi_ref[...], bi_ref[...])
    y = _dot(mg_ref[...], wo_ref[...])
    out_ref[...] = _layer_norm(alpha * h + y, g_ref[...], b_ref[...])


def _mix_ln(x, ln_in_g, ln_in_b, merged16, w_o16, g, b, alpha):
    m, d = x.shape
    tm = _pick_tile(m, 256)
    row = pl.BlockSpec((tm, d), lambda i: (i, 0))
    vec = pl.BlockSpec((1, d), lambda i: (0, 0))
    return pl.pallas_call(
        functools.partial(_mix_ln_body, alpha=alpha),
        grid=(m // tm,),
        in_specs=[row, vec, vec, row, pl.BlockSpec((d, d), lambda i: (0, 0)), vec, vec],
        out_specs=row,
        out_shape=jax.ShapeDtypeStruct((m, d), F32),
        compiler_params=_params(("parallel",)),
        name="mix_ln",
    )(x, ln_in_g.reshape(1, d), ln_in_b.reshape(1, d), merged16, w_o16, g.reshape(1, d), b.reshape(1, d))


def _ple_body(h_ref, p_ref, wpg_ref, bpg_ref, wpe_ref, g_ref, b_ref, out_ref, *, alpha):
    h = h_ref[...]
    gate = jax.nn.sigmoid(_dot(h.astype(BF16), wpg_ref[...]) + bpg_ref[...])
    e = _dot(p_ref[...].astype(BF16), wpe_ref[...])
    out_ref[...] = _layer_norm(alpha * h + gate * e, g_ref[...], b_ref[...])


def _ple(h, p, w_pg16, b_pg, w_pe16, g, b, alpha):
    m, d = h.shape
    dp = p.shape[1]
    tm = _pick_tile(m, 256)
    row = pl.BlockSpec((tm, d), lambda i: (i, 0))
    vec = pl.BlockSpec((1, d), lambda i: (0, 0))
    return pl.pallas_call(
        functools.partial(_ple_body, alpha=alpha),
        grid=(m // tm,),
        in_specs=[row, pl.BlockSpec((tm, dp), lambda i: (i, 0)),
                  pl.BlockSpec((d, d), lambda i: (0, 0)), vec,
                  pl.BlockSpec((dp, d), lambda i: (0, 0)), vec, vec],
        out_specs=row,
        out_shape=jax.ShapeDtypeStruct((m, d), F32),
        compiler_params=_params(("parallel",)),
        name="ple",
    )(h, p, w_pg16, b_pg.reshape(1, d), w_pe16, g.reshape(1, d), b.reshape(1, d))


def kernel(x_prompt, x_sample, cache_sb_k, cache_sb_v, cache_fox_k, cache_fox_v, cache_fox_logf, page_table, p_prompt, p_sample, ln_in_g, ln_in_b, w_in, b_f, b_merge, w_br_sb, w_br_fox, w_o, ln_mix_g, ln_mix_b, w_pe, w_pg, b_pg, ln_ple_g, ln_ple_b):
    depth = w_in.shape[0]
    assert depth == 1, "single-layer trunk only"
    batch, seq, d_model = x_prompt.shape
    n_seq, n_new, _ = x_sample.shape
    alpha = (2.0 * depth) ** 0.25
    assert w_in.shape[2] == 8 * HEAD_W + N_HEADS + 2 * d_model

    w = w_in[0]
    qkv_cols = jnp.concatenate([w[:, 0:3 * HEAD_W], w[:, 4 * HEAD_W:7 * HEAD_W]], axis=1).astype(BF16)
    rest_cols = jnp.concatenate([w[:, 3 * HEAD_W:4 * HEAD_W], w[:, 7 * HEAD_W:8 * HEAD_W],
                                 w[:, 8 * HEAD_W + N_HEADS:]], axis=1).astype(BF16)
    wf16 = jnp.pad(w[:, 8 * HEAD_W:8 * HEAD_W + N_HEADS], ((0, 0), (0, HEAD_DIM - N_HEADS))).astype(BF16)
    bf_pad = jnp.pad(b_f[0], (0, HEAD_DIM - N_HEADS)).reshape(1, HEAD_DIM)
    w_br_sb16 = w_br_sb[0].astype(BF16)
    w_br_fox16 = w_br_fox[0].astype(BF16)
    w_o16 = w_o[0].astype(BF16)
    w_pg16 = w_pg[0].astype(BF16)
    w_pe16 = w_pe[0].astype(BF16)

    def project(x2d):
        qkv32, qkv16 = _ln_proj(x2d, ln_in_g, ln_in_b, qkv_cols, emit16=True)
        rest32, logf = _ln_proj(x2d, ln_in_g, ln_in_b, rest_cols, emit16=False, wf16=wf16, bf=bf_pad)
        return qkv32, qkv16, rest32, logf

    def finish(x2d, o_sb, o_fx, rest32, p2d):
        merged16 = _merge(o_sb, o_fx, rest32, b_merge[0], w_br_sb16, w_br_fox16, d_model)
        h = _mix_ln(x2d, ln_in_g, ln_in_b, merged16, w_o16, ln_mix_g[0], ln_mix_b[0], alpha)
        return _ple(h, p2d, w_pg16, b_pg[0], w_pe16, ln_ple_g[0], ln_ple_b[0], alpha)

    def new_rows(qkv32, logf, lead):
        kv = [qkv32[:, c * HEAD_W:(c + 1) * HEAD_W].reshape(1, *lead, N_HEADS, HEAD_DIM) for c in (1, 2, 4, 5)]
        return kv + [logf[:, :N_HEADS].reshape(1, *lead, N_HEADS)]

    xp = x_prompt.reshape(batch * seq, d_model)
    qkv32_p, qkv16_p, rest32_p, logf_p = project(xp)
    fq, frow = _fcum(logf_p, batch, seq)
    o_sb_p, o_fx_p = _prompt_attention(qkv16_p, fq, frow, batch, seq)
    y_prompt = finish(xp, o_sb_p, o_fx_p, rest32_p, p_prompt[0].reshape(batch * seq, -1))

    xs = x_sample.reshape(n_seq * n_new, d_model)
    qkv32_s, _, rest32_s, logf_s = project(xs)

    def heads_major(a, c):
        return jnp.swapaxes(a[:, c * HEAD_W:(c + 1) * HEAD_W].reshape(n_seq, n_new, N_HEADS, HEAD_DIM), 1, 2)

    def query_rows(a, c):
        q = jnp.pad(heads_major(a, c), ((0, 0), (0, 0), (0, Q_PAD - n_new), (0, 0)))
        return q.reshape(n_seq, N_HEADS * Q_PAD, HEAD_DIM)

    def token_rows(o):
        o = o.reshape(n_seq, N_HEADS, Q_PAD, HEAD_DIM)[:, :, :n_new]
        return jnp.swapaxes(o, 1, 2).reshape(n_seq * n_new, HEAD_W)

    def key_head_rows(cache):
        n_pool, page = cache.shape[1:3]
        return cache[0].reshape(n_pool, page * N_HEADS, HEAD_DIM)

    lf_new = logf_s[:, :N_HEADS].reshape(n_seq, n_new, N_HEADS)
    lf_row = jnp.pad(jnp.swapaxes(lf_new, 1, 2), ((0, 0), (0, 0), (0, HEAD_DIM - n_new)))
    o_sb_s, o_fx_s = _sample_attention(
        page_table, query_rows(qkv32_s, 0), query_rows(qkv32_s, 3),
        heads_major(qkv32_s, 1), heads_major(qkv32_s, 2), heads_major(qkv32_s, 4), heads_major(qkv32_s, 5),
        lf_row, key_head_rows(cache_sb_k), key_head_rows(cache_sb_v), key_head_rows(cache_fox_k),
        key_head_rows(cache_fox_v), jnp.swapaxes(cache_fox_logf[0], 1, 2), pages_per_step=4)
    y_sample = finish(xs, token_rows(o_sb_s), token_rows(o_fx_s), rest32_s, p_sample[0].reshape(n_seq * n_new, -1))

    return (y_prompt.reshape(batch, seq, d_model), y_sample.reshape(n_seq, n_new, d_model),
            *new_rows(qkv32_p, logf_p, (batch, seq)), *new_rows(qkv32_s, logf_s, (n_seq, n_new)))
```

```python
import functools
import math

import jax
import jax.numpy as jnp
from jax import lax
from jax.experimental import pallas as pl
from jax.experimental.pallas import tpu as pltpu

F32 = jnp.float32
BF16 = jnp.bfloat16

HEAD_DIM = 128
N_HEADS = 8
HEAD_W = N_HEADS * HEAD_DIM
LN_EPS = 1e-5
QK_SCALE = 1.0 / math.sqrt(HEAD_DIM)
NEG_BIG = -1e30
EXP_ZERO_BELOW = -105.0
NORM_SLACK = 1.01
V7X_VMEM_LIMIT = 56 * 1024 * 1024

NT_DIMS = (((1,), (1,)), ((), ()))


def _nt_dot(a, b):
    return lax.dot_general(a, b, NT_DIMS, preferred_element_type=F32)


def _dot(a, b):
    return jnp.dot(a, b, preferred_element_type=F32)


def _log_sigmoid(x):
    return jnp.minimum(x, 0.0) - jnp.log1p(jnp.exp(-jnp.abs(x)))


def _split2(x):
    hi = x.astype(BF16)
    lo = (x - hi.astype(F32)).astype(BF16)
    return hi, lo


def _split3(x):
    hi = x.astype(BF16)
    r = x - hi.astype(F32)
    mid = r.astype(BF16)
    lo = (r - mid.astype(F32)).astype(BF16)
    return hi, mid, lo


def _layer_norm(x, g, b):
    mu = jnp.mean(x, axis=-1, keepdims=True)
    xc = x - mu
    var = jnp.mean(xc * xc, axis=-1, keepdims=True)
    return xc * lax.rsqrt(var + LN_EPS) * g + b


def _pick_tile(n, limit):
    if n <= HEAD_DIM:
        return n
    best = None
    for t in range(HEAD_DIM, min(n, limit) + 1, HEAD_DIM):
        if n % t == 0:
            best = t
    assert best is not None, (n, limit)
    return best


def _params(sem):
    return pltpu.CompilerParams(dimension_semantics=sem, vmem_limit_bytes=V7X_VMEM_LIMIT)


def _upper_strict(n):
    row = lax.broadcasted_iota(jnp.int32, (n, n), 0)
    col = lax.broadcasted_iota(jnp.int32, (n, n), 1)
    return jnp.where(row > col, 1.0, 0.0).astype(BF16)


def _rev_excl_cumsum(x, tri):
    rows = x.shape[0]
    hi, lo = _split2(x)
    both = _dot(jnp.concatenate([hi, lo], axis=0), tri)
    return both[:rows] + both[rows:]


def _ln_proj_body(*refs, emit16, aux):
    x_ref, g_ref, b_ref, w_ref = refs[:4]
    pos = 4
    if aux:
        wf_ref, bf_ref = refs[pos:pos + 2]
        pos += 2
    y32_ref = refs[pos]
    pos += 1
    if emit16:
        y16_ref = refs[pos]
        pos += 1
    if aux:
        logf_ref = refs[pos]
        pos += 1
    xs_ref = refs[pos]

    @pl.when(pl.program_id(1) == 0)
    def _():
        h = _layer_norm(x_ref[...], g_ref[...], b_ref[...])
        xs_ref[...] = h.astype(BF16)
        if aux:
            f = _dot(xs_ref[...], wf_ref[...]) + bf_ref[...]
            logf_ref[...] = _log_sigmoid(f)

    y = _dot(xs_ref[...], w_ref[...])
    y32_ref[...] = y
    if emit16:
        y16_ref[...] = y.astype(BF16)


def _ln_proj(x, g, b, w16, *, emit16, wf16=None, bf=None):
    m, d = x.shape
    n = w16.shape[1]
    tm = _pick_tile(m, 1024)
    tn = _pick_tile(n, 1024)
    aux = wf16 is not None
    in_specs = [
        pl.BlockSpec((tm, d), lambda i, j: (i, 0)),
        pl.BlockSpec((1, d), lambda i, j: (0, 0)),
        pl.BlockSpec((1, d), lambda i, j: (0, 0)),
        pl.BlockSpec((d, tn), lambda i, j: (0, j)),
    ]
    args = [x, g.reshape(1, d), b.reshape(1, d), w16]
    out_shape = [jax.ShapeDtypeStruct((m, n), F32)]
    out_specs = [pl.BlockSpec((tm, tn), lambda i, j: (i, j))]
    if emit16:
        out_shape.append(jax.ShapeDtypeStruct((m, n), BF16))
        out_specs.append(pl.BlockSpec((tm, tn), lambda i, j: (i, j)))
    if aux:
        in_specs += [pl.BlockSpec((d, HEAD_DIM), lambda i, j: (0, 0)),
                     pl.BlockSpec((1, HEAD_DIM), lambda i, j: (0, 0))]
        args += [wf16, bf]
        out_shape.append(jax.ShapeDtypeStruct((m, HEAD_DIM), F32))
        out_specs.append(pl.BlockSpec((tm, HEAD_DIM), lambda i, j: (i, 0)))
    return pl.pallas_call(
        functools.partial(_ln_proj_body, emit16=emit16, aux=aux),
        grid=(m // tm, n // tn),
        in_specs=in_specs,
        out_specs=out_specs,
        out_shape=out_shape,
        scratch_shapes=[pltpu.VMEM((tm, d), BF16)],
        compiler_params=_params(("parallel", "arbitrary")),
        name="ln_proj",
    )(*args)


def _fcum_body(lf_ref, fq_ref, frow_ref, carry_ref, *, tc):
    @pl.when(pl.program_id(1) == 0)
    def _():
        carry_ref[...] = jnp.zeros_like(carry_ref)

    row = lax.broadcasted_iota(jnp.int32, (tc, tc), 0)
    col = lax.broadcasted_iota(jnp.int32, (tc, tc), 1)
    lower_incl = jnp.where(col <= row, 1.0, 0.0).astype(BF16)
    hi, mid, lo = _split3(lf_ref[...])
    f = _dot(lower_incl, hi) + _dot(lower_incl, mid) + _dot(lower_incl, lo) + carry_ref[...]
    fq_ref[...] = f
    carry_ref[...] = f[tc - 1:tc, :]
    sel = jnp.where(lax.broadcasted_iota(jnp.int32, (N_HEADS, HEAD_DIM), 0)
                    == lax.broadcasted_iota(jnp.int32, (N_HEADS, HEAD_DIM), 1), 1.0, 0.0).astype(BF16)
    fh, fm, fl = _split3(f)
    frow_ref[...] = _nt_dot(sel, fh) + _nt_dot(sel, fm) + _nt_dot(sel, fl)


def _fcum(logf, batch, seq):
    tc = min(256, seq)
    nc = seq // tc
    return pl.pallas_call(
        functools.partial(_fcum_body, tc=tc),
        grid=(batch, nc),
        in_specs=[pl.BlockSpec((tc, HEAD_DIM), lambda b, c: (b * nc + c, 0))],
        out_specs=[pl.BlockSpec((tc, HEAD_DIM), lambda b, c: (b * nc + c, 0)),
                   pl.BlockSpec((None, N_HEADS, tc), lambda b, c: (b, 0, c))],
        out_shape=[jax.ShapeDtypeStruct((batch * seq, HEAD_DIM), F32),
                   jax.ShapeDtypeStruct((batch, N_HEADS, seq), F32)],
        scratch_shapes=[pltpu.VMEM((1, HEAD_DIM), F32)],
        compiler_params=_params(("parallel", "arbitrary")),
        name="fcum",
    )(logf)


def _sb_prompt_body(q_ref, k_ref, v_ref, o_ref, *, tq):
    i = pl.program_id(2)
    q = q_ref[...]
    tri = _upper_strict(tq)
    row = lax.broadcasted_iota(jnp.int32, (tq, tq), 0)
    col = lax.broadcasted_iota(jnp.int32, (tq, tq), 1)
    causal = col < row

    def block(j, carry, masked):
        acc, c = carry
        start = pl.multiple_of(j * tq, tq)
        k = k_ref[pl.ds(start, tq), :]
        v = v_ref[pl.ds(start, tq), :]
        z = _nt_dot(q, k) * QK_SCALE
        lsn = _log_sigmoid(-z)
        lk = jnp.where(causal, lsn, 0.0) if masked else lsn
        later = _rev_excl_cumsum(lk, tri) + c
        a = jnp.exp(z + lsn + later)
        if masked:
            a = jnp.where(causal, a, 0.0)
        acc = acc + _dot(a.astype(BF16), v)
        c = c + jnp.sum(lk, axis=-1, keepdims=True)
        return acc, c

    def live(c):
        return jnp.max(c) > EXP_ZERO_BELOW

    def cond(state):
        jj, _, _, alive = state
        return jnp.logical_and(jj < i, alive)

    def body(state):
        jj, acc, c, _ = state
        acc, c = block(i - 1 - jj, (acc, c), False)
        return jj + 1, acc, c, live(c)

    acc, c = block(i, (jnp.zeros((tq, HEAD_DIM), F32), jnp.zeros((tq, 1), F32)), True)
    _, acc, _, _ = lax.while_loop(cond, body, (jnp.int32(0), acc, c, live(c)))
    o_ref[...] = acc


FOX_CHUNK_BLOCKS = 4


def _fox_prompt_body(q_ref, k_ref, v_ref, fq_ref, frow_ref, o_ref, acc_ref, m_ref, l_ref, kmax_ref, *, tq, nq):
    h = pl.program_id(1)
    i = pl.program_id(2)

    @pl.when(i == 0)
    def _():
        k = k_ref[...].astype(F32)
        kmax_ref[...] = jnp.sqrt(jnp.max(jnp.sum(k * k, axis=-1, keepdims=True), axis=0, keepdims=True))

    q = q_ref[...]
    lane = lax.broadcasted_iota(jnp.int32, (tq, HEAD_DIM), 1)
    fq = jnp.sum(jnp.where(lane == h, fq_ref[...], 0.0), axis=-1, keepdims=True)
    qf = q.astype(F32)
    reach = jnp.sqrt(jnp.sum(qf * qf, axis=-1, keepdims=True)) * kmax_ref[...] * (QK_SCALE * NORM_SLACK) + fq

    def chunk(jb, nb, masked):
        start = pl.multiple_of(jb * tq, tq)
        k = k_ref[pl.ds(start, nb * tq), :]
        v = v_ref[pl.ds(start, nb * tq), :]
        fk = jnp.concatenate([frow_ref[h, pl.ds(jb + t, 1), :] for t in range(nb)], axis=1)
        logits = _nt_dot(q, k) * QK_SCALE + (fq - fk)
        if masked:
            q_pos = i * tq + lax.broadcasted_iota(jnp.int32, (tq, nb * tq), 0)
            k_pos = start + lax.broadcasted_iota(jnp.int32, (tq, nb * tq), 1)
            logits = jnp.where(k_pos <= q_pos, logits, NEG_BIG)
        m_old = m_ref[...]
        m_new = jnp.maximum(m_old, jnp.max(logits, axis=-1, keepdims=True))
        alpha = jnp.exp(m_old - m_new)
        p = jnp.exp(logits - m_new)
        m_ref[...] = m_new
        l_ref[...] = l_ref[...] * alpha + jnp.sum(p, axis=-1, keepdims=True)
        acc_ref[...] = acc_ref[...] * alpha + _dot(p.astype(BF16), v)

    acc_ref[...] = jnp.zeros_like(acc_ref)
    m_ref[...] = jnp.full(m_ref.shape, NEG_BIG, F32)
    l_ref[...] = jnp.zeros_like(l_ref)
    n_diag = min(FOX_CHUNK_BLOCKS, nq)
    left = jnp.maximum(i - (n_diag - 1), 0)
    chunk(left, n_diag, True)

    for nb in (1, 2):
        if nb < FOX_CHUNK_BLOCKS:
            take = jnp.bitwise_and(left, nb)
            pl.when(take != 0)(functools.partial(chunk, left - nb, nb, False))
            left = left - take

    def live(n_left):
        newest = frow_ref[h, pl.ds(jnp.maximum(n_left - 1, 0), 1), :][:, tq - 1:tq]
        return jnp.max(reach - newest - m_ref[...]) > EXP_ZERO_BELOW

    def cond(state):
        n_left, alive = state
        return jnp.logical_and(n_left > 0, alive)

    def body(state):
        n_left, _ = state
        n_left = n_left - FOX_CHUNK_BLOCKS
        chunk(n_left, FOX_CHUNK_BLOCKS, False)
        return n_left, live(n_left)

    lax.while_loop(cond, body, (left, live(left)))
    o_ref[...] = acc_ref[...] / l_ref[...]


def _prompt_attention(qkv16, fq, frow, batch, seq):
    tq = min(256, seq)
    nq = seq // tq
    m = batch * seq
    grid = (batch, N_HEADS, nq)

    def q_spec(group):
        return pl.BlockSpec((tq, HEAD_DIM), lambda b, h, i: (b * nq + i, group * N_HEADS + h))

    def kv_spec(group):
        return pl.BlockSpec((seq, HEAD_DIM), lambda b, h, i: (b, group * N_HEADS + h))

    o_spec = pl.BlockSpec((tq, HEAD_DIM), lambda b, h, i: (b * nq + i, h))
    o_shape = jax.ShapeDtypeStruct((m, HEAD_W), F32)
    sem = ("parallel", "parallel", "arbitrary")
    o_sb = pl.pallas_call(
        functools.partial(_sb_prompt_body, tq=tq),
        grid=grid,
        in_specs=[q_spec(0), kv_spec(1), kv_spec(2)],
        out_specs=o_spec, out_shape=o_shape,
        compiler_params=_params(sem), name="sb_prompt",
    )(qkv16, qkv16, qkv16)
    o_fx = pl.pallas_call(
        functools.partial(_fox_prompt_body, tq=tq, nq=nq),
        grid=grid,
        in_specs=[q_spec(3), kv_spec(4), kv_spec(5),
                  pl.BlockSpec((tq, HEAD_DIM), lambda b, h, i: (b * nq + i, 0)),
                  pl.BlockSpec((None, N_HEADS, nq, tq), lambda b, h, i: (b, 0, 0, 0))],
        out_specs=o_spec, out_shape=o_shape,
        scratch_shapes=[pltpu.VMEM((tq, HEAD_DIM), F32), pltpu.VMEM((tq, 1), F32), pltpu.VMEM((tq, 1), F32),
                        pltpu.VMEM((1, 1), F32)],
        compiler_params=_params(sem), name="fox_prompt",
    )(qkv16, qkv16, qkv16, fq, frow.reshape(batch, N_HEADS, nq, tq))
    return o_sb, o_fx


Q_PAD = 8
N_PAIRS = N_HEADS // 2
PAIR_W = 2 * HEAD_DIM


def _expand_heads(x):
    n = x.shape[1]
    return jnp.concatenate([jnp.broadcast_to(x[h:h + 1, :], (Q_PAD, n)) for h in range(N_HEADS)], axis=0)


SB_LIVE = 0
SB_FETCHED = 1
FV_STARTED = 3
N_FLAGS = 8


def _sample_body(pt_ref, qs_ref, qf_ref, ksn_ref, vsn_ref, kfn_ref, vfn_ref, lfr_ref, tri_ref,
                 csk_hbm, csv_hbm, cfk_hbm, cfv_hbm, clf_hbm, osb_ref, ofx_ref,
                 qbs_ref, qbf_ref, accs_ref, accf_ref, cs_ref, m_ref, l_ref, cg_ref, cq_ref, pad_ref,
                 skb, svb, fkb, fvb, lfb, pr_ref, alpha_ref, flags,
                 sk_sem, sv_sem, fk_sem, fv_sem, lf_sem, *, n_new, pages_per_step, page, chunk):
    g = pages_per_step
    rows = N_HEADS * Q_PAD
    s = pl.program_id(0)
    p = pl.program_id(1)
    n_seq = pl.num_programs(0)
    n_steps = pl.num_programs(1)
    cur = lax.rem(s * n_steps + p, 2)
    nxt = 1 - cur
    last = p == n_steps - 1

    def copies(hbm, buf, sem, seq, step, slot):
        return [pltpu.make_async_copy(hbm.at[pt_ref[seq, (n_steps - 1 - step) * g + gi]],
                                      buf.at[slot, gi], sem.at[slot]) for gi in range(g)]

    def start(*stream):
        for cp in copies(*stream):
            cp.start()

    def wait(*stream):
        for cp in copies(*stream):
            cp.wait()

    sk, sv, fk, fv, lf_ = ((csk_hbm, skb, sk_sem), (csv_hbm, svb, sv_sem), (cfk_hbm, fkb, fk_sem),
                           (cfv_hbm, fvb, fv_sem), (clf_hbm, lfb, lf_sem))

    @pl.when(jnp.logical_and(s == 0, p == 0))
    def _cold_start():
        for stream in (fk, lf_, sk, sv):
            start(*stream, 0, 0, 0)
        for i in range(N_FLAGS):
            flags[i] = 0
        flags[SB_FETCHED] = 1

    s_next = jnp.where(last, s + 1, s)
    p_next = jnp.where(last, 0, p + 1)

    @pl.when(s_next < n_seq)
    def _prefetch():
        start(*fk, s_next, p_next, nxt)
        start(*lf_, s_next, p_next, nxt)
        fetch_sb = jnp.logical_or(jnp.logical_or(last, p == 0), flags[SB_LIVE] != 0)
        flags[SB_FETCHED + nxt] = fetch_sb.astype(jnp.int32)

        @pl.when(fetch_sb)
        def _():
            start(*sk, s_next, p_next, nxt)
            start(*sv, s_next, p_next, nxt)

    def pair_rows(x, pp):
        return x[pp * 2 * Q_PAD:(pp + 1) * 2 * Q_PAD]

    def scores(qb_ref, key_pair):
        return jnp.concatenate(
            [_nt_dot(qb_ref[pp], key_pair(pp)) for pp in range(N_PAIRS)], axis=0) * QK_SCALE

    def cache_pair(buf, slot, page_slots):
        def head_rows(gi, h):
            return buf[slot, gi, pl.ds(h, page, stride=N_HEADS), :]

        def build(pp):
            return jnp.concatenate(
                [jnp.concatenate([head_rows(gi, 2 * pp), head_rows(gi, 2 * pp + 1)], axis=1)
                 for gi in page_slots], axis=0).astype(BF16)
        return build

    @pl.when(p == 0)
    def _init():
        zeros = jnp.zeros((Q_PAD, HEAD_DIM), F32)

        def block_diag(q_ref_, qb_ref):
            for pp in range(N_PAIRS):
                top = jnp.concatenate([q_ref_[(2 * pp) * Q_PAD:(2 * pp + 1) * Q_PAD, :], zeros], axis=1)
                bot = jnp.concatenate([zeros, q_ref_[(2 * pp + 1) * Q_PAD:(2 * pp + 2) * Q_PAD, :]], axis=1)
                qb_ref[pp] = jnp.concatenate([top, bot], axis=0).astype(BF16)

        block_diag(qs_ref, qbs_ref)
        block_diag(qf_ref, qbf_ref)

        kw = pad_ref.shape[0]
        qi = lax.broadcasted_iota(jnp.int32, (rows, kw), 0) & (Q_PAD - 1)
        ki = lax.broadcasted_iota(jnp.int32, (rows, kw), 1)
        tri = tri_ref[0:kw, 0:kw]

        def new_pair(ref):
            def build(pp):
                pad_ref[...] = jnp.zeros_like(pad_ref)
                pad_ref[0:n_new, 0:HEAD_DIM] = ref[2 * pp]
                pad_ref[0:n_new, HEAD_DIM:PAIR_W] = ref[2 * pp + 1]
                return pad_ref[...].astype(BF16)
            return build

        mask = ki < qi
        z = scores(qbs_ref, new_pair(ksn_ref))
        lsn = _log_sigmoid(-z)
        lk = jnp.where(mask, lsn, 0.0)
        later = _rev_excl_cumsum(lk, tri)
        a = jnp.where(mask, jnp.exp(z + lsn + later), 0.0).astype(BF16)
        v_pair = new_pair(vsn_ref)
        for pp in range(N_PAIRS):
            accs_ref[pp] = _dot(pair_rows(a, pp), v_pair(pp))
        cs_ref[...] = jnp.sum(lk, axis=-1, keepdims=True)

        lfr = lfr_ref[...]
        ck = jnp.sum(lfr, axis=-1, keepdims=True) - _rev_excl_cumsum(lfr, tri)
        ck = _expand_heads(ck)
        cq = jnp.sum(jnp.where(ki == qi, ck, 0.0), axis=-1, keepdims=True)
        cq_ref[...] = cq
        logits = scores(qbf_ref, new_pair(kfn_ref)) + (cq - ck)
        logits = jnp.where(ki <= qi, logits, NEG_BIG)
        m0 = jnp.max(logits, axis=-1, keepdims=True)
        pr = jnp.exp(logits - m0)
        m_ref[...] = m0
        l_ref[...] = jnp.sum(pr, axis=-1, keepdims=True)
        pr = pr.astype(BF16)
        v_pair = new_pair(vfn_ref)
        for pp in range(N_PAIRS):
            accf_ref[pp] = _dot(pair_rows(pr, pp), v_pair(pp))
        cg_ref[...] = jnp.zeros_like(cg_ref)
        flags[SB_LIVE] = 1

    tri = tri_ref[...]
    pages_per_chunk = chunk // page
    n_chunks = (g * page) // chunk
    all_pages = range(g)

    def chunk_pages(ci):
        return range(ci * pages_per_chunk, (ci + 1) * pages_per_chunk)

    wait(*fk, s, p, cur)
    wait(*lf_, s, p, cur)
    parts = [None] * n_chunks
    for ci in reversed(range(n_chunks)):
        lf = jnp.concatenate([lfb[cur, gi] for gi in chunk_pages(ci)], axis=1)
        gl = _rev_excl_cumsum(lf, tri) + cg_ref[...]
        parts[ci] = scores(qbf_ref, cache_pair(fkb, cur, chunk_pages(ci))) + (_expand_heads(gl) + cq_ref[...])
        cg_ref[...] += jnp.sum(lf, axis=-1, keepdims=True)
    logits = jnp.concatenate(parts, axis=1)
    m_old = m_ref[...]
    row_max = jnp.max(logits, axis=-1, keepdims=True)
    m_new = jnp.maximum(m_old, row_max)
    alpha = jnp.exp(m_old - m_new)
    pr = jnp.exp(logits - m_new)
    m_ref[...] = m_new
    l_ref[...] = l_ref[...] * alpha + jnp.sum(pr, axis=-1, keepdims=True)
    need_values = jnp.max(row_max - m_new) > EXP_ZERO_BELOW

    def apply_values(slot):
        a_slot = alpha_ref[slot]
        pr_slot = pr_ref[slot]
        v_pair = cache_pair(fvb, slot, all_pages)
        for pp in range(N_PAIRS):
            accf_ref[pp] = accf_ref[pp] * pair_rows(a_slot, pp) + _dot(pair_rows(pr_slot, pp), v_pair(pp))

    @pl.when(jnp.logical_and(p > 0, flags[FV_STARTED + nxt] != 0))
    def _previous_group_values():
        wait(*fv, s, p - 1, nxt)
        apply_values(nxt)
        flags[FV_STARTED + nxt] = 0

    @pl.when(need_values)
    def _request_values():
        pr_ref[cur] = pr.astype(BF16)
        alpha_ref[cur] = alpha
        start(*fv, s, p, cur)
        flags[FV_STARTED + cur] = 1

    @pl.when(flags[SB_FETCHED + cur] != 0)
    def _():
        wait(*sk, s, p, cur)
        wait(*sv, s, p, cur)

    @pl.when(jnp.logical_and(flags[SB_FETCHED + cur] != 0, flags[SB_LIVE] != 0))
    def _stick_breaking():
        for ci in reversed(range(n_chunks)):
            z = scores(qbs_ref, cache_pair(skb, cur, chunk_pages(ci)))
            lsn = _log_sigmoid(-z)
            later = _rev_excl_cumsum(lsn, tri) + cs_ref[...]
            a = jnp.exp(z + lsn + later).astype(BF16)
            v_pair = cache_pair(svb, cur, chunk_pages(ci))
            for pp in range(N_PAIRS):
                accs_ref[pp] += _dot(pair_rows(a, pp), v_pair(pp))
            cs_ref[...] += jnp.sum(lsn, axis=-1, keepdims=True)
        real_row = (lax.broadcasted_iota(jnp.int32, (rows, 1), 0) & (Q_PAD - 1)) < n_new
        flags[SB_LIVE] = (jnp.max(jnp.where(real_row, cs_ref[...], NEG_BIG)) > EXP_ZERO_BELOW).astype(jnp.int32)

    @pl.when(last)
    def _fin():
        @pl.when(flags[FV_STARTED + cur] != 0)
        def _():
            wait(*fv, s, p, cur)
            apply_values(cur)
            flags[FV_STARTED + cur] = 0

        inv_l = 1.0 / l_ref[...]
        for h in range(N_HEADS):
            pp, odd = divmod(h, 2)
            r0, c0 = odd * Q_PAD, odd * HEAD_DIM
            osb_ref[h * Q_PAD:(h + 1) * Q_PAD, :] = accs_ref[pp, r0:r0 + Q_PAD, c0:c0 + HEAD_DIM]
            ofx_ref[h * Q_PAD:(h + 1) * Q_PAD, :] = (accf_ref[pp, r0:r0 + Q_PAD, c0:c0 + HEAD_DIM]
                                                     * inv_l[h * Q_PAD:(h + 1) * Q_PAD, :])


def _sample_attention(page_table, q_sb, q_fx, k_sb, v_sb, k_fx, v_fx, lf_row,
                      c_sb_k, c_sb_v, c_fx_k, c_fx_v, c_lf_t, *, pages_per_step):
    n_seq, _, n_new, _ = k_sb.shape
    n_pages = page_table.shape[1]
    page = c_sb_k.shape[1] // N_HEADS
    g = pages_per_step
    assert n_pages % g == 0 and n_new <= Q_PAD
    n_steps = n_pages // g
    rows = N_HEADS * Q_PAD
    chunk = min(2 * page, g * page)
    assert (g * page) % chunk == 0
    tri = _upper_strict(chunk)

    def seq_spec(shape):
        return pl.BlockSpec((None,) + shape, lambda s, p, pt: (s,) + (0,) * len(shape))

    in_specs = ([seq_spec((rows, HEAD_DIM))] * 2 + [seq_spec((N_HEADS, n_new, HEAD_DIM))] * 4
                + [seq_spec((N_HEADS, HEAD_DIM)), pl.BlockSpec((chunk, chunk), lambda s, p, pt: (0, 0))]
                + [pl.BlockSpec(memory_space=pl.ANY)] * 5)
    args = [q_sb, q_fx, k_sb, v_sb, k_fx, v_fx, lf_row, tri, c_sb_k, c_sb_v, c_fx_k, c_fx_v, c_lf_t]
    page_buf = pltpu.VMEM((2, g, page * N_HEADS, HEAD_DIM), F32)
    dma_sem = pltpu.SemaphoreType.DMA((2,))

    grid_spec = pltpu.PrefetchScalarGridSpec(
        num_scalar_prefetch=1,
        grid=(n_seq, n_steps),
        in_specs=in_specs,
        out_specs=[seq_spec((rows, HEAD_DIM))] * 2,
        scratch_shapes=[
            pltpu.VMEM((N_PAIRS, 2 * Q_PAD, PAIR_W), BF16), pltpu.VMEM((N_PAIRS, 2 * Q_PAD, PAIR_W), BF16),
            pltpu.VMEM((N_PAIRS, 2 * Q_PAD, PAIR_W), F32), pltpu.VMEM((N_PAIRS, 2 * Q_PAD, PAIR_W), F32),
            pltpu.VMEM((rows, 1), F32), pltpu.VMEM((rows, 1), F32), pltpu.VMEM((rows, 1), F32),
            pltpu.VMEM((N_HEADS, 1), F32), pltpu.VMEM((rows, 1), F32),
            pltpu.VMEM((HEAD_DIM, PAIR_W), F32),
            page_buf, page_buf, page_buf, page_buf, pltpu.VMEM((2, g, N_HEADS, page), F32),
            pltpu.VMEM((2, rows, g * page), BF16), pltpu.VMEM((2, rows, 1), F32),
            pltpu.SMEM((N_FLAGS,), jnp.int32),
            dma_sem, dma_sem, dma_sem, dma_sem, dma_sem,
        ],
    )
    return pl.pallas_call(
        functools.partial(_sample_body, n_new=n_new, pages_per_step=g, page=page, chunk=chunk),
        grid_spec=grid_spec,
        out_shape=[jax.ShapeDtypeStruct((n_seq, rows, HEAD_DIM), F32)] * 2,
        compiler_params=_params(("arbitrary", "arbitrary")),
        name="sample_attn",
    )(page_table, *args)


def _merge_body(osb_ref, ofx_ref, zsb_ref, zfx_ref, gsb_ref, gfx_ref, bm_ref, wsb_ref, wfx_ref, out_ref):
    def branch(o_ref, z_ref, w_ref):
        z = z_ref[...]
        a = o_ref[...] * (z * jax.nn.sigmoid(z))
        return _dot(a.astype(BF16), w_ref[...])

    bm = bm_ref[...]
    merged = (jax.nn.sigmoid(gsb_ref[...] + bm[0:1, :]) * branch(osb_ref, zsb_ref, wsb_ref)
              + jax.nn.sigmoid(gfx_ref[...] + bm[1:2, :]) * branch(ofx_ref, zfx_ref, wfx_ref))
    out_ref[...] = merged.astype(BF16)


def _merge(o_sb, o_fx, rest32, b_merge, w_br_sb16, w_br_fox16, d_model):
    m = o_sb.shape[0]
    tm = _pick_tile(m, 512)
    tn = _pick_tile(math.gcd(d_model, HEAD_W), 1024)
    g_off = 2 * HEAD_W // tn
    nd = d_model // tn
    row_spec = pl.BlockSpec((tm, HEAD_W), lambda i, j: (i, 0))
    return pl.pallas_call(
        _merge_body,
        grid=(m // tm, nd),
        in_specs=[row_spec, row_spec,
                  pl.BlockSpec((tm, HEAD_W), lambda i, j: (i, 0)),
                  pl.BlockSpec((tm, HEAD_W), lambda i, j: (i, 1)),
                  pl.BlockSpec((tm, tn), lambda i, j: (i, g_off + j)),
                  pl.BlockSpec((tm, tn), lambda i, j: (i, g_off + nd + j)),
                  pl.BlockSpec((2, tn), lambda i, j: (0, j)),
                  pl.BlockSpec((HEAD_W, tn), lambda i, j: (0, j)),
                  pl.BlockSpec((HEAD_W, tn), lambda i, j: (0, j))],
        out_specs=pl.BlockSpec((tm, tn), lambda i, j: (i, j)),
        out_shape=jax.ShapeDtypeStruct((m, d_model), BF16),
        compiler_params=_params(("parallel", "arbitrary")),
        name="merge",
    )(o_sb, o_fx, rest32, rest32, rest32, rest32, b_merge, w_br_sb16, w_br_fox16)


def _mix_ln_body(x_ref, gi_ref, bi_ref, mg_ref, wo_ref, g_ref, b_ref, out_ref, *, alpha):
    h = _layer_norm(x_ref[...], gi_ref[...], bi_ref[...])
    y = _dot(mg_ref[...], wo_ref[...])
    out_ref[...] = _layer_norm(alpha * h + y, g_ref[...], b_ref[...])


def _mix_ln(x, ln_in_g, ln_in_b, merged16, w_o16, g, b, alpha):
    m, d = x.shape
    tm = _pick_tile(m, 256)
    row = pl.BlockSpec((tm, d), lambda i: (i, 0))
    vec = pl.BlockSpec((1, d), lambda i: (0, 0))
    return pl.pallas_call(
        functools.partial(_mix_ln_body, alpha=alpha),
        grid=(m // tm,),
        in_specs=[row, vec, vec, row, pl.BlockSpec((d, d), lambda i: (0, 0)), vec, vec],
        out_specs=row,
        out_shape=jax.ShapeDtypeStruct((m, d), F32),
        compiler_params=_params(("parallel",)),
        name="mix_ln",
    )(x, ln_in_g.reshape(1, d), ln_in_b.reshape(1, d), merged16, w_o16, g.reshape(1, d), b.reshape(1, d))


def _ple_body(h_ref, p_ref, wpg_ref, bpg_ref, wpe_ref, g_ref, b_ref, out_ref, *, alpha):
    h = h_ref[...]
    gate = jax.nn.sigmoid(_dot(h.astype(BF16), wpg_ref[...]) + bpg_ref[...])
    e = _dot(p_ref[...].astype(BF16), wpe_ref[...])
    out_ref[...] = _layer_norm(alpha * h + gate * e, g_ref[...], b_ref[...])


def _ple(h, p, w_pg16, b_pg, w_pe16, g, b, alpha):
    m, d = h.shape
    dp = p.shape[1]
    tm = _pick_tile(m, 256)
    row = pl.BlockSpec((tm, d), lambda i: (i, 0))
    vec = pl.BlockSpec((1, d), lambda i: (0, 0))
    return pl.pallas_call(
        functools.partial(_ple_body, alpha=alpha),
        grid=(m // tm,),
        in_specs=[row, pl.BlockSpec((tm, dp), lambda i: (i, 0)),
                  pl.BlockSpec((d, d), lambda i: (0, 0)), vec,
                  pl.BlockSpec((dp, d), lambda i: (0, 0)), vec, vec],
        out_specs=row,
        out_shape=jax.ShapeDtypeStruct((m, d), F32),
        compiler_params=_params(("parallel",)),
        name="ple",
    )(h, p, w_pg16, b_pg.reshape(1, d), w_pe16, g.reshape(1, d), b.reshape(1, d))


def kernel(x_prompt, x_sample, cache_sb_k, cache_sb_v, cache_fox_k, cache_fox_v, cache_fox_logf, page_table, p_prompt, p_sample, ln_in_g, ln_in_b, w_in, b_f, b_merge, w_br_sb, w_br_fox, w_o, ln_mix_g, ln_mix_b, w_pe, w_pg, b_pg, ln_ple_g, ln_ple_b):
    depth = w_in.shape[0]
    assert depth == 1, "single-layer trunk only"
    batch, seq, d_model = x_prompt.shape
    n_seq, n_new, _ = x_sample.shape
    alpha = (2.0 * depth) ** 0.25
    assert w_in.shape[2] == 8 * HEAD_W + N_HEADS + 2 * d_model

    w = w_in[0]
    qkv_cols = jnp.concatenate([w[:, 0:3 * HEAD_W], w[:, 4 * HEAD_W:7 * HEAD_W]], axis=1).astype(BF16)
    rest_cols = jnp.concatenate([w[:, 3 * HEAD_W:4 * HEAD_W], w[:, 7 * HEAD_W:8 * HEAD_W],
                                 w[:, 8 * HEAD_W + N_HEADS:]], axis=1).astype(BF16)
    wf16 = jnp.pad(w[:, 8 * HEAD_W:8 * HEAD_W + N_HEADS], ((0, 0), (0, HEAD_DIM - N_HEADS))).astype(BF16)
    bf_pad = jnp.pad(b_f[0], (0, HEAD_DIM - N_HEADS)).reshape(1, HEAD_DIM)
    w_br_sb16 = w_br_sb[0].astype(BF16)
    w_br_fox16 = w_br_fox[0].astype(BF16)
    w_o16 = w_o[0].astype(BF16)
    w_pg16 = w_pg[0].astype(BF16)
    w_pe16 = w_pe[0].astype(BF16)

    def project(x2d):
        qkv32, qkv16 = _ln_proj(x2d, ln_in_g, ln_in_b, qkv_cols, emit16=True)
        rest32, logf = _ln_proj(x2d, ln_in_g, ln_in_b, rest_cols, emit16=False, wf16=wf16, bf=bf_pad)
        return qkv32, qkv16, rest32, logf

    def finish(x2d, o_sb, o_fx, rest32, p2d):
        merged16 = _merge(o_sb, o_fx, rest32, b_merge[0], w_br_sb16, w_br_fox16, d_model)
        h = _mix_ln(x2d, ln_in_g, ln_in_b, merged16, w_o16, ln_mix_g[0], ln_mix_b[0], alpha)
        return _ple(h, p2d, w_pg16, b_pg[0], w_pe16, ln_ple_g[0], ln_ple_b[0], alpha)

    def new_rows(qkv32, logf, lead):
        kv = [qkv32[:, c * HEAD_W:(c + 1) * HEAD_W].reshape(1, *lead, N_HEADS, HEAD_DIM) for c in (1, 2, 4, 5)]
        return kv + [logf[:, :N_HEADS].reshape(1, *lead, N_HEADS)]

    xp = x_prompt.reshape(batch * seq, d_model)
    qkv32_p, qkv16_p, rest32_p, logf_p = project(xp)
    fq, frow = _fcum(logf_p, batch, seq)
    o_sb_p, o_fx_p = _prompt_attention(qkv16_p, fq, frow, batch, seq)
    y_prompt = finish(xp, o_sb_p, o_fx_p, rest32_p, p_prompt[0].reshape(batch * seq, -1))

    xs = x_sample.reshape(n_seq * n_new, d_model)
    qkv32_s, _, rest32_s, logf_s = project(xs)

    def heads_major(a, c):
        return jnp.swapaxes(a[:, c * HEAD_W:(c + 1) * HEAD_W].reshape(n_seq, n_new, N_HEADS, HEAD_DIM), 1, 2)

    def query_rows(a, c):
        q = jnp.pad(heads_major(a, c), ((0, 0), (0, 0), (0, Q_PAD - n_new), (0, 0)))
        return q.reshape(n_seq, N_HEADS * Q_PAD, HEAD_DIM)

    def token_rows(o):
        o = o.reshape(n_seq, N_HEADS, Q_PAD, HEAD_DIM)[:, :, :n_new]
        return jnp.swapaxes(o, 1, 2).reshape(n_seq * n_new, HEAD_W)

    def key_head_rows(cache):
        n_pool, page = cache.shape[1:3]
        return cache[0].reshape(n_pool, page * N_HEADS, HEAD_DIM)

    lf_new = logf_s[:, :N_HEADS].reshape(n_seq, n_new, N_HEADS)
    lf_row = jnp.pad(jnp.swapaxes(lf_new, 1, 2), ((0, 0), (0, 0), (0, HEAD_DIM - n_new)))
    o_sb_s, o_fx_s = _sample_attention(
        page_table, query_rows(qkv32_s, 0), query_rows(qkv32_s, 3),
        heads_major(qkv32_s, 1), heads_major(qkv32_s, 2), heads_major(qkv32_s, 4), heads_major(qkv32_s, 5),
        lf_row, key_head_rows(cache_sb_k), key_head_rows(cache_sb_v), key_head_rows(cache_fox_k),
        key_head_rows(cache_fox_v), jnp.swapaxes(cache_fox_logf[0], 1, 2),
        pages_per_step=math.gcd(page_table.shape[1], 8))
    y_sample = finish(xs, token_rows(o_sb_s), token_rows(o_fx_s), rest32_s, p_sample[0].reshape(n_seq * n_new, -1))

    return (y_prompt.reshape(batch, seq, d_model), y_sample.reshape(n_seq, n_new, d_model),
            *new_rows(qkv32_p, logf_p, (batch, seq)), *new_rows(qkv32_s, logf_s, (n_seq, n_new)))
```

```python
import functools
import math

import jax
import jax.numpy as jnp
from jax import lax
from jax.experimental import pallas as pl
from jax.experimental.pallas import tpu as pltpu

F32 = jnp.float32
BF16 = jnp.bfloat16

HEAD_DIM = 128
N_HEADS = 8
HEAD_W = N_HEADS * HEAD_DIM
LN_EPS = 1e-5
QK_SCALE = 1.0 / math.sqrt(HEAD_DIM)
NEG_BIG = -1e30
EXP_ZERO_BELOW = -105.0
NORM_SLACK = 1.01
V7X_VMEM_LIMIT = 56 * 1024 * 1024

NT_DIMS = (((1,), (1,)), ((), ()))


def _nt_dot(a, b):
    return lax.dot_general(a, b, NT_DIMS, preferred_element_type=F32)


def _dot(a, b):
    return jnp.dot(a, b, preferred_element_type=F32)


def _log_sigmoid(x):
    return jnp.minimum(x, 0.0) - jnp.log1p(jnp.exp(-jnp.abs(x)))


def _split2(x):
    hi = x.astype(BF16)
    lo = (x - hi.astype(F32)).astype(BF16)
    return hi, lo


def _split3(x):
    hi = x.astype(BF16)
    r = x - hi.astype(F32)
    mid = r.astype(BF16)
    lo = (r - mid.astype(F32)).astype(BF16)
    return hi, mid, lo


def _layer_norm(x, g, b):
    mu = jnp.mean(x, axis=-1, keepdims=True)
    xc = x - mu
    var = jnp.mean(xc * xc, axis=-1, keepdims=True)
    return xc * lax.rsqrt(var + LN_EPS) * g + b


def _pick_tile(n, limit):
    if n <= HEAD_DIM:
        return n
    best = None
    for t in range(HEAD_DIM, min(n, limit) + 1, HEAD_DIM):
        if n % t == 0:
            best = t
    assert best is not None, (n, limit)
    return best


def _params(sem):
    return pltpu.CompilerParams(dimension_semantics=sem, vmem_limit_bytes=V7X_VMEM_LIMIT)


def _upper_strict(n):
    row = lax.broadcasted_iota(jnp.int32, (n, n), 0)
    col = lax.broadcasted_iota(jnp.int32, (n, n), 1)
    return jnp.where(row > col, 1.0, 0.0).astype(BF16)


def _rev_excl_cumsum(x, tri):
    rows = x.shape[0]
    hi, lo = _split2(x)
    both = _dot(jnp.concatenate([hi, lo], axis=0), tri)
    return both[:rows] + both[rows:]


def _ln_proj_body(*refs, emit32, emit16, aux, head_tiles, tm):
    x_ref, g_ref, b_ref, w_ref = refs[:4]
    pos = 4
    if aux:
        wf_ref, bf_ref = refs[pos:pos + 2]
        pos += 2
    if emit32:
        y32_ref = refs[pos]
        pos += 1
    if emit16:
        y16_ref = refs[pos]
        pos += 1
    head_refs = refs[pos:pos + len(head_tiles)]
    pos += len(head_tiles)
    if aux:
        logf_ref = refs[pos]
        pos += 1
    xs_ref = refs[pos]
    j = pl.program_id(1)

    @pl.when(j == 0)
    def _():
        h = _layer_norm(x_ref[...], g_ref[...], b_ref[...])
        xs_ref[...] = h.astype(BF16)
        if aux:
            f = _dot(xs_ref[...], wf_ref[...]) + bf_ref[...]
            logf_ref[...] = _log_sigmoid(f)

    y = _dot(xs_ref[...], w_ref[...])
    if emit32:
        y32_ref[...] = y
    if emit16:
        y16_ref[...] = y.astype(BF16)
    for tile, ref in zip(head_tiles, head_refs):
        @pl.when(j == tile)
        def _(ref=ref):
            for h in range(N_HEADS):
                ref[pl.ds(h, tm, stride=N_HEADS), :] = y[:, h * HEAD_DIM:(h + 1) * HEAD_DIM]


def _ln_proj(x, g, b, w16, *, tm_limit, emit32, emit16, head_tiles=(), wf16=None, bf=None):
    m, d = x.shape
    n = w16.shape[1]
    tm = _pick_tile(m, tm_limit)
    tn = _pick_tile(n, HEAD_W)
    assert not head_tiles or tn == HEAD_W
    aux = wf16 is not None
    in_specs = [
        pl.BlockSpec((tm, d), lambda i, j: (i, 0)),
        pl.BlockSpec((1, d), lambda i, j: (0, 0)),
        pl.BlockSpec((1, d), lambda i, j: (0, 0)),
        pl.BlockSpec((d, tn), lambda i, j: (0, j)),
    ]
    args = [x, g.reshape(1, d), b.reshape(1, d), w16]
    out_shape, out_specs = [], []
    if emit32:
        out_shape.append(jax.ShapeDtypeStruct((m, n), F32))
        out_specs.append(pl.BlockSpec((tm, tn), lambda i, j: (i, j)))
    if emit16:
        out_shape.append(jax.ShapeDtypeStruct((m, n), BF16))
        out_specs.append(pl.BlockSpec((tm, tn), lambda i, j: (i, j)))
    for _ in head_tiles:
        out_shape.append(jax.ShapeDtypeStruct((m * N_HEADS, HEAD_DIM), F32))
        out_specs.append(pl.BlockSpec((tm * N_HEADS, HEAD_DIM), lambda i, j: (i, 0)))
    if aux:
        in_specs += [pl.BlockSpec((d, HEAD_DIM), lambda i, j: (0, 0)),
                     pl.BlockSpec((1, HEAD_DIM), lambda i, j: (0, 0))]
        args += [wf16, bf]
        out_shape.append(jax.ShapeDtypeStruct((m, HEAD_DIM), F32))
        out_specs.append(pl.BlockSpec((tm, HEAD_DIM), lambda i, j: (i, 0)))
    return pl.pallas_call(
        functools.partial(_ln_proj_body, emit32=emit32, emit16=emit16, aux=aux, head_tiles=tuple(head_tiles),
                          tm=tm),
        grid=(m // tm, n // tn),
        in_specs=in_specs,
        out_specs=out_specs,
        out_shape=out_shape,
        scratch_shapes=[pltpu.VMEM((tm, d), BF16)],
        compiler_params=_params(("parallel", "arbitrary")),
        name="ln_proj",
    )(*args)


def _fcum_body(lf_ref, fq_ref, frow_ref, carry_ref, *, tc):
    @pl.when(pl.program_id(1) == 0)
    def _():
        carry_ref[...] = jnp.zeros_like(carry_ref)

    row = lax.broadcasted_iota(jnp.int32, (tc, tc), 0)
    col = lax.broadcasted_iota(jnp.int32, (tc, tc), 1)
    lower_incl = jnp.where(col <= row, 1.0, 0.0).astype(BF16)
    hi, mid, lo = _split3(lf_ref[...])
    f = _dot(lower_incl, hi) + _dot(lower_incl, mid) + _dot(lower_incl, lo) + carry_ref[...]
    fq_ref[...] = f
    carry_ref[...] = f[tc - 1:tc, :]
    sel = jnp.where(lax.broadcasted_iota(jnp.int32, (N_HEADS, HEAD_DIM), 0)
                    == lax.broadcasted_iota(jnp.int32, (N_HEADS, HEAD_DIM), 1), 1.0, 0.0).astype(BF16)
    fh, fm, fl = _split3(f)
    frow_ref[...] = _nt_dot(sel, fh) + _nt_dot(sel, fm) + _nt_dot(sel, fl)


def _fcum(logf, batch, seq):
    tc = min(256, seq)
    nc = seq // tc
    return pl.pallas_call(
        functools.partial(_fcum_body, tc=tc),
        grid=(batch, nc),
        in_specs=[pl.BlockSpec((tc, HEAD_DIM), lambda b, c: (b * nc + c, 0))],
        out_specs=[pl.BlockSpec((tc, HEAD_DIM), lambda b, c: (b * nc + c, 0)),
                   pl.BlockSpec((None, N_HEADS, tc), lambda b, c: (b, 0, c))],
        out_shape=[jax.ShapeDtypeStruct((batch * seq, HEAD_DIM), F32),
                   jax.ShapeDtypeStruct((batch, N_HEADS, seq), F32)],
        scratch_shapes=[pltpu.VMEM((1, HEAD_DIM), F32)],
        compiler_params=_params(("parallel", "arbitrary")),
        name="fcum",
    )(logf)


def _sb_prompt_body(q_ref, k_ref, v_ref, o_ref, *, tq):
    i = pl.program_id(2)
    q = q_ref[...]
    tri = _upper_strict(tq)
    row = lax.broadcasted_iota(jnp.int32, (tq, tq), 0)
    col = lax.broadcasted_iota(jnp.int32, (tq, tq), 1)
    causal = col < row

    def block(j, carry, masked):
        acc, c = carry
        start = pl.multiple_of(j * tq, tq)
        k = k_ref[pl.ds(start, tq), :]
        v = v_ref[pl.ds(start, tq), :]
        z = _nt_dot(q, k) * QK_SCALE
        lsn = _log_sigmoid(-z)
        lk = jnp.where(causal, lsn, 0.0) if masked else lsn
        later = _rev_excl_cumsum(lk, tri) + c
        a = jnp.exp(z + lsn + later)
        if masked:
            a = jnp.where(causal, a, 0.0)
        acc = acc + _dot(a.astype(BF16), v)
        c = c + jnp.sum(lk, axis=-1, keepdims=True)
        return acc, c

    def live(c):
        return jnp.max(c) > EXP_ZERO_BELOW

    def cond(state):
        jj, _, _, alive = state
        return jnp.logical_and(jj < i, alive)

    def body(state):
        jj, acc, c, _ = state
        acc, c = block(i - 1 - jj, (acc, c), False)
        return jj + 1, acc, c, live(c)

    acc, c = block(i, (jnp.zeros((tq, HEAD_DIM), F32), jnp.zeros((tq, 1), F32)), True)
    _, acc, _, _ = lax.while_loop(cond, body, (jnp.int32(0), acc, c, live(c)))
    o_ref[...] = acc


FOX_CHUNK_BLOCKS = 4


def _fox_prompt_body(q_ref, k_ref, v_ref, fq_ref, frow_ref, o_ref, acc_ref, m_ref, l_ref, kmax_ref, *, tq, nq):
    h = pl.program_id(1)
    i = pl.program_id(2)

    @pl.when(i == 0)
    def _():
        k = k_ref[...].astype(F32)
        kmax_ref[...] = jnp.sqrt(jnp.max(jnp.sum(k * k, axis=-1, keepdims=True), axis=0, keepdims=True))

    q = q_ref[...]
    lane = lax.broadcasted_iota(jnp.int32, (tq, HEAD_DIM), 1)
    fq = jnp.sum(jnp.where(lane == h, fq_ref[...], 0.0), axis=-1, keepdims=True)
    qf = q.astype(F32)
    reach = jnp.sqrt(jnp.sum(qf * qf, axis=-1, keepdims=True)) * kmax_ref[...] * (QK_SCALE * NORM_SLACK) + fq

    def chunk(jb, nb, masked):
        start = pl.multiple_of(jb * tq, tq)
        k = k_ref[pl.ds(start, nb * tq), :]
        v = v_ref[pl.ds(start, nb * tq), :]
        fk = jnp.concatenate([frow_ref[h, pl.ds(jb + t, 1), :] for t in range(nb)], axis=1)
        logits = _nt_dot(q, k) * QK_SCALE + (fq - fk)
        if masked:
            q_pos = i * tq + lax.broadcasted_iota(jnp.int32, (tq, nb * tq), 0)
            k_pos = start + lax.broadcasted_iota(jnp.int32, (tq, nb * tq), 1)
            logits = jnp.where(k_pos <= q_pos, logits, NEG_BIG)
        m_old = m_ref[...]
        m_new = jnp.maximum(m_old, jnp.max(logits, axis=-1, keepdims=True))
        alpha = jnp.exp(m_old - m_new)
        p = jnp.exp(logits - m_new)
        m_ref[...] = m_new
        l_ref[...] = l_ref[...] * alpha + jnp.sum(p, axis=-1, keepdims=True)
        acc_ref[...] = acc_ref[...] * alpha + _dot(p.astype(BF16), v)

    acc_ref[...] = jnp.zeros_like(acc_ref)
    m_ref[...] = jnp.full(m_ref.shape, NEG_BIG, F32)
    l_ref[...] = jnp.zeros_like(l_ref)
    n_diag = min(FOX_CHUNK_BLOCKS, nq)
    left = jnp.maximum(i - (n_diag - 1), 0)
    chunk(left, n_diag, True)

    for nb in (1, 2):
        if nb < FOX_CHUNK_BLOCKS:
            take = jnp.bitwise_and(left, nb)
            pl.when(take != 0)(functools.partial(chunk, left - nb, nb, False))
            left = left - take

    def live(n_left):
        newest = frow_ref[h, pl.ds(jnp.maximum(n_left - 1, 0), 1), :][:, tq - 1:tq]
        return jnp.max(reach - newest - m_ref[...]) > EXP_ZERO_BELOW

    def cond(state):
        n_left, alive = state
        return jnp.logical_and(n_left > 0, alive)

    def body(state):
        n_left, _ = state
        n_left = n_left - FOX_CHUNK_BLOCKS
        chunk(n_left, FOX_CHUNK_BLOCKS, False)
        return n_left, live(n_left)

    lax.while_loop(cond, body, (left, live(left)))
    o_ref[...] = acc_ref[...] / l_ref[...]


def _prompt_attention(qkv16, fq, frow, batch, seq):
    tq = min(256, seq)
    nq = seq // tq
    m = batch * seq
    grid = (batch, N_HEADS, nq)

    def q_spec(group):
        return pl.BlockSpec((tq, HEAD_DIM), lambda b, h, i: (b * nq + i, group * N_HEADS + h))

    def kv_spec(group):
        return pl.BlockSpec((seq, HEAD_DIM), lambda b, h, i: (b, group * N_HEADS + h))

    o_spec = pl.BlockSpec((tq, HEAD_DIM), lambda b, h, i: (b * nq + i, h))
    o_shape = jax.ShapeDtypeStruct((m, HEAD_W), F32)
    sem = ("parallel", "parallel", "arbitrary")
    o_sb = pl.pallas_call(
        functools.partial(_sb_prompt_body, tq=tq),
        grid=grid,
        in_specs=[q_spec(0), kv_spec(1), kv_spec(2)],
        out_specs=o_spec, out_shape=o_shape,
        compiler_params=_params(sem), name="sb_prompt",
    )(qkv16, qkv16, qkv16)
    o_fx = pl.pallas_call(
        functools.partial(_fox_prompt_body, tq=tq, nq=nq),
        grid=grid,
        in_specs=[q_spec(3), kv_spec(4), kv_spec(5),
                  pl.BlockSpec((tq, HEAD_DIM), lambda b, h, i: (b * nq + i, 0)),
                  pl.BlockSpec((None, N_HEADS, nq, tq), lambda b, h, i: (b, 0, 0, 0))],
        out_specs=o_spec, out_shape=o_shape,
        scratch_shapes=[pltpu.VMEM((tq, HEAD_DIM), F32), pltpu.VMEM((tq, 1), F32), pltpu.VMEM((tq, 1), F32),
                        pltpu.VMEM((1, 1), F32)],
        compiler_params=_params(sem), name="fox_prompt",
    )(qkv16, qkv16, qkv16, fq, frow.reshape(batch, N_HEADS, nq, tq))
    return o_sb, o_fx


Q_PAD = 8
N_PAIRS = N_HEADS // 2
PAIR_W = 2 * HEAD_DIM


def _expand_heads(x):
    n = x.shape[1]
    return jnp.concatenate([jnp.broadcast_to(x[h:h + 1, :], (Q_PAD, n)) for h in range(N_HEADS)], axis=0)


SB_LIVE = 0
SB_FETCHED = 1
FV_STARTED = 3
N_FLAGS = 8


def _sample_body(pt_ref, qs_ref, qf_ref, ksn_ref, vsn_ref, kfn_ref, vfn_ref, lfr_ref, tri_ref,
                 csk_hbm, csv_hbm, cfk_hbm, cfv_hbm, clf_hbm, osb_ref, ofx_ref,
                 qbs_ref, qbf_ref, accs_ref, accf_ref, cs_ref, m_ref, l_ref, cg_ref, cq_ref, pad_ref,
                 skb, svb, fkb, fvb, lfb, pr_ref, alpha_ref, flags,
                 sk_sem, sv_sem, fk_sem, fv_sem, lf_sem, *, n_new, pages_per_step, page, chunk):
    g = pages_per_step
    rows = N_HEADS * Q_PAD
    s = pl.program_id(0)
    p = pl.program_id(1)
    n_seq = pl.num_programs(0)
    n_steps = pl.num_programs(1)
    cur = lax.rem(s * n_steps + p, 2)
    nxt = 1 - cur
    last = p == n_steps - 1

    def copies(hbm, buf, sem, seq, step, slot):
        return [pltpu.make_async_copy(hbm.at[pt_ref[seq, (n_steps - 1 - step) * g + gi]],
                                      buf.at[slot, gi], sem.at[slot]) for gi in range(g)]

    def start(*stream):
        for cp in copies(*stream):
            cp.start()

    def wait(*stream):
        for cp in copies(*stream):
            cp.wait()

    sk, sv, fk, fv, lf_ = ((csk_hbm, skb, sk_sem), (csv_hbm, svb, sv_sem), (cfk_hbm, fkb, fk_sem),
                           (cfv_hbm, fvb, fv_sem), (clf_hbm, lfb, lf_sem))

    @pl.when(jnp.logical_and(s == 0, p == 0))
    def _cold_start():
        for stream in (fk, lf_, sk, sv):
            start(*stream, 0, 0, 0)
        for i in range(N_FLAGS):
            flags[i] = 0
        flags[SB_FETCHED] = 1

    s_next = jnp.where(last, s + 1, s)
    p_next = jnp.where(last, 0, p + 1)

    @pl.when(s_next < n_seq)
    def _prefetch():
        start(*fk, s_next, p_next, nxt)
        start(*lf_, s_next, p_next, nxt)
        fetch_sb = jnp.logical_or(jnp.logical_or(last, p == 0), flags[SB_LIVE] != 0)
        flags[SB_FETCHED + nxt] = fetch_sb.astype(jnp.int32)

        @pl.when(fetch_sb)
        def _():
            start(*sk, s_next, p_next, nxt)
            start(*sv, s_next, p_next, nxt)

    def pair_rows(x, pp):
        return x[pp * 2 * Q_PAD:(pp + 1) * 2 * Q_PAD]

    def scores(qb_ref, key_pair):
        return jnp.concatenate(
            [_nt_dot(qb_ref[pp], key_pair(pp)) for pp in range(N_PAIRS)], axis=0) * QK_SCALE

    def cache_pair(buf, slot, page_slots):
        def head_rows(gi, h):
            return buf[slot, gi, pl.ds(h, page, stride=N_HEADS), :]

        def build(pp):
            return jnp.concatenate(
                [jnp.concatenate([head_rows(gi, 2 * pp), head_rows(gi, 2 * pp + 1)], axis=1)
                 for gi in page_slots], axis=0).astype(BF16)
        return build

    @pl.when(p == 0)
    def _init():
        zeros = jnp.zeros((Q_PAD, HEAD_DIM), F32)

        def block_diag(q_ref_, qb_ref):
            for pp in range(N_PAIRS):
                top = jnp.concatenate([q_ref_[(2 * pp) * Q_PAD:(2 * pp + 1) * Q_PAD, :], zeros], axis=1)
                bot = jnp.concatenate([zeros, q_ref_[(2 * pp + 1) * Q_PAD:(2 * pp + 2) * Q_PAD, :]], axis=1)
                qb_ref[pp] = jnp.concatenate([top, bot], axis=0).astype(BF16)

        block_diag(qs_ref, qbs_ref)
        block_diag(qf_ref, qbf_ref)

        kw = pad_ref.shape[0]
        qi = lax.broadcasted_iota(jnp.int32, (rows, kw), 0) & (Q_PAD - 1)
        ki = lax.broadcasted_iota(jnp.int32, (rows, kw), 1)
        tri = tri_ref[0:kw, 0:kw]

        def new_pair(ref):
            def build(pp):
                pad_ref[...] = jnp.zeros_like(pad_ref)
                pad_ref[0:n_new, 0:HEAD_DIM] = ref[2 * pp]
                pad_ref[0:n_new, HEAD_DIM:PAIR_W] = ref[2 * pp + 1]
                return pad_ref[...].astype(BF16)
            return build

        mask = ki < qi
        z = scores(qbs_ref, new_pair(ksn_ref))
        lsn = _log_sigmoid(-z)
        lk = jnp.where(mask, lsn, 0.0)
        later = _rev_excl_cumsum(lk, tri)
        a = jnp.where(mask, jnp.exp(z + lsn + later), 0.0).astype(BF16)
        v_pair = new_pair(vsn_ref)
        for pp in range(N_PAIRS):
            accs_ref[pp] = _dot(pair_rows(a, pp), v_pair(pp))
        cs_ref[...] = jnp.sum(lk, axis=-1, keepdims=True)

        lfr = lfr_ref[...]
        ck = jnp.sum(lfr, axis=-1, keepdims=True) - _rev_excl_cumsum(lfr, tri)
        ck = _expand_heads(ck)
        cq = jnp.sum(jnp.where(ki == qi, ck, 0.0), axis=-1, keepdims=True)
        cq_ref[...] = cq
        logits = scores(qbf_ref, new_pair(kfn_ref)) + (cq - ck)
        logits = jnp.where(ki <= qi, logits, NEG_BIG)
        m0 = jnp.max(logits, axis=-1, keepdims=True)
        pr = jnp.exp(logits - m0)
        m_ref[...] = m0
        l_ref[...] = jnp.sum(pr, axis=-1, keepdims=True)
        pr = pr.astype(BF16)
        v_pair = new_pair(vfn_ref)
        for pp in range(N_PAIRS):
            accf_ref[pp] = _dot(pair_rows(pr, pp), v_pair(pp))
        cg_ref[...] = jnp.zeros_like(cg_ref)
        flags[SB_LIVE] = 1

    tri = tri_ref[...]
    pages_per_chunk = chunk // page
    n_chunks = (g * page) // chunk
    all_pages = range(g)

    def chunk_pages(ci):
        return range(ci * pages_per_chunk, (ci + 1) * pages_per_chunk)

    wait(*fk, s, p, cur)
    wait(*lf_, s, p, cur)
    parts = [None] * n_chunks
    for ci in reversed(range(n_chunks)):
        lf = jnp.concatenate([lfb[cur, gi] for gi in chunk_pages(ci)], axis=1)
        gl = _rev_excl_cumsum(lf, tri) + cg_ref[...]
        parts[ci] = scores(qbf_ref, cache_pair(fkb, cur, chunk_pages(ci))) + (_expand_heads(gl) + cq_ref[...])
        cg_ref[...] += jnp.sum(lf, axis=-1, keepdims=True)
    logits = jnp.concatenate(parts, axis=1)
    m_old = m_ref[...]
    row_max = jnp.max(logits, axis=-1, keepdims=True)
    m_new = jnp.maximum(m_old, row_max)
    alpha = jnp.exp(m_old - m_new)
    pr = jnp.exp(logits - m_new)
    m_ref[...] = m_new
    l_ref[...] = l_ref[...] * alpha + jnp.sum(pr, axis=-1, keepdims=True)
    need_values = jnp.max(row_max - m_new) > EXP_ZERO_BELOW

    def apply_values(slot):
        a_slot = alpha_ref[slot]
        pr_slot = pr_ref[slot]
        v_pair = cache_pair(fvb, slot, all_pages)
        for pp in range(N_PAIRS):
            accf_ref[pp] = accf_ref[pp] * pair_rows(a_slot, pp) + _dot(pair_rows(pr_slot, pp), v_pair(pp))

    @pl.when(jnp.logical_and(p > 0, flags[FV_STARTED + nxt] != 0))
    def _previous_group_values():
        wait(*fv, s, p - 1, nxt)
        apply_values(nxt)
        flags[FV_STARTED + nxt] = 0

    @pl.when(need_values)
    def _request_values():
        pr_ref[cur] = pr.astype(BF16)
        alpha_ref[cur] = alpha
        start(*fv, s, p, cur)
        flags[FV_STARTED + cur] = 1

    @pl.when(flags[SB_FETCHED + cur] != 0)
    def _():
        wait(*sk, s, p, cur)
        wait(*sv, s, p, cur)

    @pl.when(jnp.logical_and(flags[SB_FETCHED + cur] != 0, flags[SB_LIVE] != 0))
    def _stick_breaking():
        for ci in reversed(range(n_chunks)):
            z = scores(qbs_ref, cache_pair(skb, cur, chunk_pages(ci)))
            lsn = _log_sigmoid(-z)
            later = _rev_excl_cumsum(lsn, tri) + cs_ref[...]
            a = jnp.exp(z + lsn + later).astype(BF16)
            v_pair = cache_pair(svb, cur, chunk_pages(ci))
            for pp in range(N_PAIRS):
                accs_ref[pp] += _dot(pair_rows(a, pp), v_pair(pp))
            cs_ref[...] += jnp.sum(lsn, axis=-1, keepdims=True)
        real_row = (lax.broadcasted_iota(jnp.int32, (rows, 1), 0) & (Q_PAD - 1)) < n_new
        flags[SB_LIVE] = (jnp.max(jnp.where(real_row, cs_ref[...], NEG_BIG)) > EXP_ZERO_BELOW).astype(jnp.int32)

    @pl.when(last)
    def _fin():
        @pl.when(flags[FV_STARTED + cur] != 0)
        def _():
            wait(*fv, s, p, cur)
            apply_values(cur)
            flags[FV_STARTED + cur] = 0

        inv_l = 1.0 / l_ref[...]
        for h in range(N_HEADS):
            pp, odd = divmod(h, 2)
            r0, c0 = odd * Q_PAD, odd * HEAD_DIM
            osb_ref[h * Q_PAD:(h + 1) * Q_PAD, :] = accs_ref[pp, r0:r0 + Q_PAD, c0:c0 + HEAD_DIM]
            ofx_ref[h * Q_PAD:(h + 1) * Q_PAD, :] = (accf_ref[pp, r0:r0 + Q_PAD, c0:c0 + HEAD_DIM]
                                                     * inv_l[h * Q_PAD:(h + 1) * Q_PAD, :])


def _sample_attention(page_table, q_sb, q_fx, k_sb, v_sb, k_fx, v_fx, lf_row,
                      c_sb_k, c_sb_v, c_fx_k, c_fx_v, c_lf_t, *, pages_per_step):
    n_seq, _, n_new, _ = k_sb.shape
    n_pages = page_table.shape[1]
    page = c_sb_k.shape[1] // N_HEADS
    g = pages_per_step
    assert n_pages % g == 0 and n_new <= Q_PAD
    n_steps = n_pages // g
    rows = N_HEADS * Q_PAD
    chunk = min(2 * page, g * page)
    assert (g * page) % chunk == 0
    tri = _upper_strict(chunk)

    def seq_spec(shape):
        return pl.BlockSpec((None,) + shape, lambda s, p, pt: (s,) + (0,) * len(shape))

    in_specs = ([seq_spec((rows, HEAD_DIM))] * 2 + [seq_spec((N_HEADS, n_new, HEAD_DIM))] * 4
                + [seq_spec((N_HEADS, HEAD_DIM)), pl.BlockSpec((chunk, chunk), lambda s, p, pt: (0, 0))]
                + [pl.BlockSpec(memory_space=pl.ANY)] * 5)
    args = [q_sb, q_fx, k_sb, v_sb, k_fx, v_fx, lf_row, tri, c_sb_k, c_sb_v, c_fx_k, c_fx_v, c_lf_t]
    page_buf = pltpu.VMEM((2, g, page * N_HEADS, HEAD_DIM), F32)
    dma_sem = pltpu.SemaphoreType.DMA((2,))

    grid_spec = pltpu.PrefetchScalarGridSpec(
        num_scalar_prefetch=1,
        grid=(n_seq, n_steps),
        in_specs=in_specs,
        out_specs=[seq_spec((rows, HEAD_DIM))] * 2,
        scratch_shapes=[
            pltpu.VMEM((N_PAIRS, 2 * Q_PAD, PAIR_W), BF16), pltpu.VMEM((N_PAIRS, 2 * Q_PAD, PAIR_W), BF16),
            pltpu.VMEM((N_PAIRS, 2 * Q_PAD, PAIR_W), F32), pltpu.VMEM((N_PAIRS, 2 * Q_PAD, PAIR_W), F32),
            pltpu.VMEM((rows, 1), F32), pltpu.VMEM((rows, 1), F32), pltpu.VMEM((rows, 1), F32),
            pltpu.VMEM((N_HEADS, 1), F32), pltpu.VMEM((rows, 1), F32),
            pltpu.VMEM((HEAD_DIM, PAIR_W), F32),
            page_buf, page_buf, page_buf, page_buf, pltpu.VMEM((2, g, N_HEADS, page), F32),
            pltpu.VMEM((2, rows, g * page), BF16), pltpu.VMEM((2, rows, 1), F32),
            pltpu.SMEM((N_FLAGS,), jnp.int32),
            dma_sem, dma_sem, dma_sem, dma_sem, dma_sem,
        ],
    )
    return pl.pallas_call(
        functools.partial(_sample_body, n_new=n_new, pages_per_step=g, page=page, chunk=chunk),
        grid_spec=grid_spec,
        out_shape=[jax.ShapeDtypeStruct((n_seq, rows, HEAD_DIM), F32)] * 2,
        compiler_params=_params(("arbitrary", "arbitrary")),
        name="sample_attn",
    )(page_table, *args)


def _merge_body(osb_ref, ofx_ref, zsb_ref, zfx_ref, gsb_ref, gfx_ref, bm_ref, wsb_ref, wfx_ref, out_ref):
    def branch(o_ref, z_ref, w_ref):
        z = z_ref[...]
        a = o_ref[...] * (z * jax.nn.sigmoid(z))
        return _dot(a.astype(BF16), w_ref[...])

    bm = bm_ref[...]
    merged = (jax.nn.sigmoid(gsb_ref[...] + bm[0:1, :]) * branch(osb_ref, zsb_ref, wsb_ref)
              + jax.nn.sigmoid(gfx_ref[...] + bm[1:2, :]) * branch(ofx_ref, zfx_ref, wfx_ref))
    out_ref[...] = merged.astype(BF16)


def _merge(o_sb, o_fx, rest32, b_merge, w_br_sb16, w_br_fox16, d_model):
    m = o_sb.shape[0]
    tm = _pick_tile(m, 512)
    tn = _pick_tile(math.gcd(d_model, HEAD_W), 1024)
    g_off = 2 * HEAD_W // tn
    nd = d_model // tn
    row_spec = pl.BlockSpec((tm, HEAD_W), lambda i, j: (i, 0))
    return pl.pallas_call(
        _merge_body,
        grid=(m // tm, nd),
        in_specs=[row_spec, row_spec,
                  pl.BlockSpec((tm, HEAD_W), lambda i, j: (i, 0)),
                  pl.BlockSpec((tm, HEAD_W), lambda i, j: (i, 1)),
                  pl.BlockSpec((tm, tn), lambda i, j: (i, g_off + j)),
                  pl.BlockSpec((tm, tn), lambda i, j: (i, g_off + nd + j)),
                  pl.BlockSpec((2, tn), lambda i, j: (0, j)),
                  pl.BlockSpec((HEAD_W, tn), lambda i, j: (0, j)),
                  pl.BlockSpec((HEAD_W, tn), lambda i, j: (0, j))],
        out_specs=pl.BlockSpec((tm, tn), lambda i, j: (i, j)),
        out_shape=jax.ShapeDtypeStruct((m, d_model), BF16),
        compiler_params=_params(("parallel", "arbitrary")),
        name="merge",
    )(o_sb, o_fx, rest32, rest32, rest32, rest32, b_merge, w_br_sb16, w_br_fox16)


def _mix_ln_body(x_ref, gi_ref, bi_ref, mg_ref, wo_ref, g_ref, b_ref, out_ref, *, alpha):
    h = _layer_norm(x_ref[...], gi_ref[...], bi_ref[...])
    y = _dot(mg_ref[...], wo_ref[...])
    out_ref[...] = _layer_norm(alpha * h + y, g_ref[...], b_ref[...])


def _mix_ln(x, ln_in_g, ln_in_b, merged16, w_o16, g, b, alpha):
    m, d = x.shape
    tm = _pick_tile(m, 256)
    row = pl.BlockSpec((tm, d), lambda i: (i, 0))
    vec = pl.BlockSpec((1, d), lambda i: (0, 0))
    return pl.pallas_call(
        functools.partial(_mix_ln_body, alpha=alpha),
        grid=(m // tm,),
        in_specs=[row, vec, vec, row, pl.BlockSpec((d, d), lambda i: (0, 0)), vec, vec],
        out_specs=row,
        out_shape=jax.ShapeDtypeStruct((m, d), F32),
        compiler_params=_params(("parallel",)),
        name="mix_ln",
    )(x, ln_in_g.reshape(1, d), ln_in_b.reshape(1, d), merged16, w_o16, g.reshape(1, d), b.reshape(1, d))


def _ple_body(h_ref, p_ref, wpg_ref, bpg_ref, wpe_ref, g_ref, b_ref, out_ref, *, alpha):
    h = h_ref[...]
    gate = jax.nn.sigmoid(_dot(h.astype(BF16), wpg_ref[...]) + bpg_ref[...])
    e = _dot(p_ref[...].astype(BF16), wpe_ref[...])
    out_ref[...] = _layer_norm(alpha * h + gate * e, g_ref[...], b_ref[...])


def _ple(h, p, w_pg16, b_pg, w_pe16, g, b, alpha):
    m, d = h.shape
    dp = p.shape[1]
    tm = _pick_tile(m, 256)
    row = pl.BlockSpec((tm, d), lambda i: (i, 0))
    vec = pl.BlockSpec((1, d), lambda i: (0, 0))
    return pl.pallas_call(
        functools.partial(_ple_body, alpha=alpha),
        grid=(m // tm,),
        in_specs=[row, pl.BlockSpec((tm, dp), lambda i: (i, 0)),
                  pl.BlockSpec((d, d), lambda i: (0, 0)), vec,
                  pl.BlockSpec((dp, d), lambda i: (0, 0)), vec, vec],
        out_specs=row,
        out_shape=jax.ShapeDtypeStruct((m, d), F32),
        compiler_params=_params(("parallel",)),
        name="ple",
    )(h, p, w_pg16, b_pg.reshape(1, d), w_pe16, g.reshape(1, d), b.reshape(1, d))


def kernel(x_prompt, x_sample, cache_sb_k, cache_sb_v, cache_fox_k, cache_fox_v, cache_fox_logf, page_table, p_prompt, p_sample, ln_in_g, ln_in_b, w_in, b_f, b_merge, w_br_sb, w_br_fox, w_o, ln_mix_g, ln_mix_b, w_pe, w_pg, b_pg, ln_ple_g, ln_ple_b):
    depth = w_in.shape[0]
    assert depth == 1, "single-layer trunk only"
    batch, seq, d_model = x_prompt.shape
    n_seq, n_new, _ = x_sample.shape
    alpha = (2.0 * depth) ** 0.25
    assert w_in.shape[2] == 8 * HEAD_W + N_HEADS + 2 * d_model

    w = w_in[0]
    qkv_cols = jnp.concatenate([w[:, 0:3 * HEAD_W], w[:, 4 * HEAD_W:7 * HEAD_W]], axis=1).astype(BF16)
    rest_cols = jnp.concatenate([w[:, 3 * HEAD_W:4 * HEAD_W], w[:, 7 * HEAD_W:8 * HEAD_W],
                                 w[:, 8 * HEAD_W + N_HEADS:]], axis=1).astype(BF16)
    wf16 = jnp.pad(w[:, 8 * HEAD_W:8 * HEAD_W + N_HEADS], ((0, 0), (0, HEAD_DIM - N_HEADS))).astype(BF16)
    bf_pad = jnp.pad(b_f[0], (0, HEAD_DIM - N_HEADS)).reshape(1, HEAD_DIM)
    w_br_sb16 = w_br_sb[0].astype(BF16)
    w_br_fox16 = w_br_fox[0].astype(BF16)
    w_o16 = w_o[0].astype(BF16)
    w_pg16 = w_pg[0].astype(BF16)
    w_pe16 = w_pe[0].astype(BF16)

    def project(x2d):
        qkv16, *kv = _ln_proj(x2d, ln_in_g, ln_in_b, qkv_cols, tm_limit=512, emit32=False, emit16=True,
                              head_tiles=(1, 2, 4, 5))
        rest32, logf = _ln_proj(x2d, ln_in_g, ln_in_b, rest_cols, tm_limit=1024, emit32=True, emit16=False,
                                wf16=wf16, bf=bf_pad)
        return qkv16, kv, rest32, logf

    def finish(x2d, o_sb, o_fx, rest32, p2d):
        merged16 = _merge(o_sb, o_fx, rest32, b_merge[0], w_br_sb16, w_br_fox16, d_model)
        h = _mix_ln(x2d, ln_in_g, ln_in_b, merged16, w_o16, ln_mix_g[0], ln_mix_b[0], alpha)
        return _ple(h, p2d, w_pg16, b_pg[0], w_pe16, ln_ple_g[0], ln_ple_b[0], alpha)

    def new_rows(kv, logf, lead):
        return ([a.reshape(1, *lead, N_HEADS, HEAD_DIM) for a in kv]
                + [logf[:, :N_HEADS].reshape(1, *lead, N_HEADS)])

    xp = x_prompt.reshape(batch * seq, d_model)
    qkv16_p, kv_p, rest32_p, logf_p = project(xp)
    fq, frow = _fcum(logf_p, batch, seq)
    o_sb_p, o_fx_p = _prompt_attention(qkv16_p, fq, frow, batch, seq)
    y_prompt = finish(xp, o_sb_p, o_fx_p, rest32_p, p_prompt[0].reshape(batch * seq, -1))

    xs = x_sample.reshape(n_seq * n_new, d_model)
    qkv16_s, kv_s, rest32_s, logf_s = project(xs)

    def heads_major(a):
        return jnp.swapaxes(a.reshape(n_seq, n_new, N_HEADS, HEAD_DIM), 1, 2)

    def query_rows(c):
        q = heads_major(qkv16_s[:, c * HEAD_W:(c + 1) * HEAD_W].astype(F32))
        q = jnp.pad(q, ((0, 0), (0, 0), (0, Q_PAD - n_new), (0, 0)))
        return q.reshape(n_seq, N_HEADS * Q_PAD, HEAD_DIM)

    def token_rows(o):
        o = o.reshape(n_seq, N_HEADS, Q_PAD, HEAD_DIM)[:, :, :n_new]
        return jnp.swapaxes(o, 1, 2).reshape(n_seq * n_new, HEAD_W)

    def key_head_rows(cache):
        n_pool, page = cache.shape[1:3]
        return cache[0].reshape(n_pool, page * N_HEADS, HEAD_DIM)

    lf_new = logf_s[:, :N_HEADS].reshape(n_seq, n_new, N_HEADS)
    lf_row = jnp.pad(jnp.swapaxes(lf_new, 1, 2), ((0, 0), (0, 0), (0, HEAD_DIM - n_new)))
    o_sb_s, o_fx_s = _sample_attention(
        page_table, query_rows(0), query_rows(3), *[heads_major(a) for a in kv_s],
        lf_row, key_head_rows(cache_sb_k), key_head_rows(cache_sb_v), key_head_rows(cache_fox_k),
        key_head_rows(cache_fox_v), jnp.swapaxes(cache_fox_logf[0], 1, 2),
        pages_per_step=math.gcd(page_table.shape[1], 8))
    y_sample = finish(xs, token_rows(o_sb_s), token_rows(o_fx_s), rest32_s, p_sample[0].reshape(n_seq * n_new, -1))

    return (y_prompt.reshape(batch, seq, d_model), y_sample.reshape(n_seq, n_new, d_model),
            *new_rows(kv_p, logf_p, (batch, seq)), *new_rows(kv_s, logf_s, (n_seq, n_new)))
```

```python
import functools
import math

import jax
import jax.numpy as jnp
from jax import lax
from jax.experimental import pallas as pl
from jax.experimental.pallas import tpu as pltpu

F32 = jnp.float32
BF16 = jnp.bfloat16

HEAD_DIM = 128
N_HEADS = 8
HEAD_W = N_HEADS * HEAD_DIM
LN_EPS = 1e-5
QK_SCALE = 1.0 / math.sqrt(HEAD_DIM)
NEG_BIG = -1e30
EXP_ZERO_BELOW = -105.0
NORM_SLACK = 1.01
V7X_VMEM_LIMIT = 56 * 1024 * 1024

NT_DIMS = (((1,), (1,)), ((), ()))


def _nt_dot(a, b):
    return lax.dot_general(a, b, NT_DIMS, preferred_element_type=F32)


def _dot(a, b):
    return jnp.dot(a, b, preferred_element_type=F32)


def _log_sigmoid(x):
    return jnp.minimum(x, 0.0) - jnp.log1p(jnp.exp(-jnp.abs(x)))


def _split2(x):
    hi = x.astype(BF16)
    lo = (x - hi.astype(F32)).astype(BF16)
    return hi, lo


def _split3(x):
    hi = x.astype(BF16)
    r = x - hi.astype(F32)
    mid = r.astype(BF16)
    lo = (r - mid.astype(F32)).astype(BF16)
    return hi, mid, lo


def _layer_norm(x, g, b):
    mu = jnp.mean(x, axis=-1, keepdims=True)
    xc = x - mu
    var = jnp.mean(xc * xc, axis=-1, keepdims=True)
    return xc * lax.rsqrt(var + LN_EPS) * g + b


def _pick_tile(n, limit):
    if n <= HEAD_DIM:
        return n
    best = None
    for t in range(HEAD_DIM, min(n, limit) + 1, HEAD_DIM):
        if n % t == 0:
            best = t
    assert best is not None, (n, limit)
    return best


def _params(sem):
    return pltpu.CompilerParams(dimension_semantics=sem, vmem_limit_bytes=V7X_VMEM_LIMIT)


def _upper_strict(n):
    row = lax.broadcasted_iota(jnp.int32, (n, n), 0)
    col = lax.broadcasted_iota(jnp.int32, (n, n), 1)
    return jnp.where(row > col, 1.0, 0.0).astype(BF16)


def _rev_excl_cumsum(x, tri):
    rows = x.shape[0]
    hi, lo = _split2(x)
    both = _dot(jnp.concatenate([hi, lo], axis=0), tri)
    return both[:rows] + both[rows:]


def _ln_proj_body(*refs, emit32, emit16, aux, head_tiles, tm):
    x_ref, g_ref, b_ref, w_ref = refs[:4]
    pos = 4
    if aux:
        wf_ref, bf_ref = refs[pos:pos + 2]
        pos += 2
    if emit32:
        y32_ref = refs[pos]
        pos += 1
    if emit16:
        y16_ref = refs[pos]
        pos += 1
    head_refs = refs[pos:pos + len(head_tiles)]
    pos += len(head_tiles)
    if aux:
        logf_ref = refs[pos]
        pos += 1
    xs_ref = refs[pos]
    j = pl.program_id(1)

    @pl.when(j == 0)
    def _():
        h = _layer_norm(x_ref[...], g_ref[...], b_ref[...])
        xs_ref[...] = h.astype(BF16)
        if aux:
            f = _dot(xs_ref[...], wf_ref[...]) + bf_ref[...]
            logf_ref[...] = _log_sigmoid(f)

    y = _dot(xs_ref[...], w_ref[...])
    if emit32:
        y32_ref[...] = y
    if emit16:
        y16_ref[...] = y.astype(BF16)
    for tile, ref in zip(head_tiles, head_refs):
        @pl.when(j == tile)
        def _(ref=ref):
            for h in range(N_HEADS):
                ref[pl.ds(h, tm, stride=N_HEADS), :] = y[:, h * HEAD_DIM:(h + 1) * HEAD_DIM]


def _ln_proj(x, g, b, w16, *, tm_limit, emit32, emit16, head_tiles=(), wf16=None, bf=None):
    m, d = x.shape
    n = w16.shape[1]
    tm = _pick_tile(m, tm_limit)
    tn = _pick_tile(n, HEAD_W)
    assert not head_tiles or tn == HEAD_W
    aux = wf16 is not None
    in_specs = [
        pl.BlockSpec((tm, d), lambda i, j: (i, 0)),
        pl.BlockSpec((1, d), lambda i, j: (0, 0)),
        pl.BlockSpec((1, d), lambda i, j: (0, 0)),
        pl.BlockSpec((d, tn), lambda i, j: (0, j)),
    ]
    args = [x, g.reshape(1, d), b.reshape(1, d), w16]
    out_shape, out_specs = [], []
    if emit32:
        out_shape.append(jax.ShapeDtypeStruct((m, n), F32))
        out_specs.append(pl.BlockSpec((tm, tn), lambda i, j: (i, j)))
    if emit16:
        out_shape.append(jax.ShapeDtypeStruct((m, n), BF16))
        out_specs.append(pl.BlockSpec((tm, tn), lambda i, j: (i, j)))
    for _ in head_tiles:
        out_shape.append(jax.ShapeDtypeStruct((m * N_HEADS, HEAD_DIM), F32))
        out_specs.append(pl.BlockSpec((tm * N_HEADS, HEAD_DIM), lambda i, j: (i, 0)))
    if aux:
        in_specs += [pl.BlockSpec((d, HEAD_DIM), lambda i, j: (0, 0)),
                     pl.BlockSpec((1, HEAD_DIM), lambda i, j: (0, 0))]
        args += [wf16, bf]
        out_shape.append(jax.ShapeDtypeStruct((m, HEAD_DIM), F32))
        out_specs.append(pl.BlockSpec((tm, HEAD_DIM), lambda i, j: (i, 0)))
    return pl.pallas_call(
        functools.partial(_ln_proj_body, emit32=emit32, emit16=emit16, aux=aux, head_tiles=tuple(head_tiles),
                          tm=tm),
        grid=(m // tm, n // tn),
        in_specs=in_specs,
        out_specs=out_specs,
        out_shape=out_shape,
        scratch_shapes=[pltpu.VMEM((tm, d), BF16)],
        compiler_params=_params(("parallel", "arbitrary")),
        name="ln_proj",
    )(*args)


def _fcum_body(lf_ref, fq_ref, frow_ref, carry_ref, *, tc):
    @pl.when(pl.program_id(1) == 0)
    def _():
        carry_ref[...] = jnp.zeros_like(carry_ref)

    row = lax.broadcasted_iota(jnp.int32, (tc, tc), 0)
    col = lax.broadcasted_iota(jnp.int32, (tc, tc), 1)
    lower_incl = jnp.where(col <= row, 1.0, 0.0).astype(BF16)
    hi, mid, lo = _split3(lf_ref[...])
    f = _dot(lower_incl, hi) + _dot(lower_incl, mid) + _dot(lower_incl, lo) + carry_ref[...]
    fq_ref[...] = f
    carry_ref[...] = f[tc - 1:tc, :]
    sel = jnp.where(lax.broadcasted_iota(jnp.int32, (N_HEADS, HEAD_DIM), 0)
                    == lax.broadcasted_iota(jnp.int32, (N_HEADS, HEAD_DIM), 1), 1.0, 0.0).astype(BF16)
    fh, fm, fl = _split3(f)
    frow_ref[...] = _nt_dot(sel, fh) + _nt_dot(sel, fm) + _nt_dot(sel, fl)


def _fcum(logf, batch, seq):
    tc = min(256, seq)
    nc = seq // tc
    return pl.pallas_call(
        functools.partial(_fcum_body, tc=tc),
        grid=(batch, nc),
        in_specs=[pl.BlockSpec((tc, HEAD_DIM), lambda b, c: (b * nc + c, 0))],
        out_specs=[pl.BlockSpec((tc, HEAD_DIM), lambda b, c: (b * nc + c, 0)),
                   pl.BlockSpec((None, N_HEADS, tc), lambda b, c: (b, 0, c))],
        out_shape=[jax.ShapeDtypeStruct((batch * seq, HEAD_DIM), F32),
                   jax.ShapeDtypeStruct((batch, N_HEADS, seq), F32)],
        scratch_shapes=[pltpu.VMEM((1, HEAD_DIM), F32)],
        compiler_params=_params(("parallel", "arbitrary")),
        name="fcum",
    )(logf)


def _sb_prompt_body(q_ref, k_ref, v_ref, o_ref, *, tq):
    i = pl.program_id(2)
    q = q_ref[...]
    tri = _upper_strict(tq)
    row = lax.broadcasted_iota(jnp.int32, (tq, tq), 0)
    col = lax.broadcasted_iota(jnp.int32, (tq, tq), 1)
    causal = col < row

    def block(j, carry, masked):
        acc, c = carry
        start = pl.multiple_of(j * tq, tq)
        k = k_ref[pl.ds(start, tq), :]
        v = v_ref[pl.ds(start, tq), :]
        z = _nt_dot(q, k) * QK_SCALE
        lsn = _log_sigmoid(-z)
        lk = jnp.where(causal, lsn, 0.0) if masked else lsn
        later = _rev_excl_cumsum(lk, tri) + c
        a = jnp.exp(z + lsn + later)
        if masked:
            a = jnp.where(causal, a, 0.0)
        acc = acc + _dot(a.astype(BF16), v)
        c = c + jnp.sum(lk, axis=-1, keepdims=True)
        return acc, c

    def live(c):
        return jnp.max(c) > EXP_ZERO_BELOW

    def cond(state):
        jj, _, _, alive = state
        return jnp.logical_and(jj < i, alive)

    def body(state):
        jj, acc, c, _ = state
        acc, c = block(i - 1 - jj, (acc, c), False)
        return jj + 1, acc, c, live(c)

    acc, c = block(i, (jnp.zeros((tq, HEAD_DIM), F32), jnp.zeros((tq, 1), F32)), True)
    _, acc, _, _ = lax.while_loop(cond, body, (jnp.int32(0), acc, c, live(c)))
    o_ref[...] = acc


SB_Q_BLOCK = 256
FOX_Q_BLOCK = 256
FOX_CHUNK_BLOCKS = 4


def _fox_prompt_body(q_ref, k_ref, v_ref, fq_ref, frow_ref, o_ref, acc_ref, m_ref, l_ref, kmax_ref, *, tq, nq):
    h = pl.program_id(1)
    i = pl.program_id(2)

    @pl.when(i == 0)
    def _():
        k = k_ref[...].astype(F32)
        kmax_ref[...] = jnp.sqrt(jnp.max(jnp.sum(k * k, axis=-1, keepdims=True), axis=0, keepdims=True))

    q = q_ref[...]
    lane = lax.broadcasted_iota(jnp.int32, (tq, HEAD_DIM), 1)
    fq = jnp.sum(jnp.where(lane == h, fq_ref[...], 0.0), axis=-1, keepdims=True)
    qf = q.astype(F32)
    reach = jnp.sqrt(jnp.sum(qf * qf, axis=-1, keepdims=True)) * kmax_ref[...] * (QK_SCALE * NORM_SLACK) + fq

    def chunk(jb, nb, masked):
        start = pl.multiple_of(jb * tq, tq)
        k = k_ref[pl.ds(start, nb * tq), :]
        v = v_ref[pl.ds(start, nb * tq), :]
        fk = jnp.concatenate([frow_ref[h, pl.ds(jb + t, 1), :] for t in range(nb)], axis=1)
        logits = _nt_dot(q, k) * QK_SCALE + (fq - fk)
        if masked:
            q_pos = i * tq + lax.broadcasted_iota(jnp.int32, (tq, nb * tq), 0)
            k_pos = start + lax.broadcasted_iota(jnp.int32, (tq, nb * tq), 1)
            logits = jnp.where(k_pos <= q_pos, logits, NEG_BIG)
        m_old = m_ref[...]
        m_new = jnp.maximum(m_old, jnp.max(logits, axis=-1, keepdims=True))
        alpha = jnp.exp(m_old - m_new)
        p = jnp.exp(logits - m_new)
        m_ref[...] = m_new
        l_ref[...] = l_ref[...] * alpha + jnp.sum(p, axis=-1, keepdims=True)
        acc_ref[...] = acc_ref[...] * alpha + _dot(p.astype(BF16), v)

    acc_ref[...] = jnp.zeros_like(acc_ref)
    m_ref[...] = jnp.full(m_ref.shape, NEG_BIG, F32)
    l_ref[...] = jnp.zeros_like(l_ref)
    n_diag = min(FOX_CHUNK_BLOCKS, nq)
    left = jnp.maximum(i - (n_diag - 1), 0)
    chunk(left, n_diag, True)

    for nb in (1, 2):
        if nb < FOX_CHUNK_BLOCKS:
            take = jnp.bitwise_and(left, nb)
            pl.when(take != 0)(functools.partial(chunk, left - nb, nb, False))
            left = left - take

    def live(n_left):
        newest = frow_ref[h, pl.ds(jnp.maximum(n_left - 1, 0), 1), :][:, tq - 1:tq]
        return jnp.max(reach - newest - m_ref[...]) > EXP_ZERO_BELOW

    def cond(state):
        n_left, alive = state
        return jnp.logical_and(n_left > 0, alive)

    def body(state):
        n_left, _ = state
        n_left = n_left - FOX_CHUNK_BLOCKS
        chunk(n_left, FOX_CHUNK_BLOCKS, False)
        return n_left, live(n_left)

    lax.while_loop(cond, body, (left, live(left)))
    o_ref[...] = acc_ref[...] / l_ref[...]


def _prompt_attention(qkv16, fq, frow, batch, seq):
    m = batch * seq

    def q_spec(group, tq):
        nq = seq // tq
        return pl.BlockSpec((tq, HEAD_DIM), lambda b, h, i: (b * nq + i, group * N_HEADS + h))

    def kv_spec(group):
        return pl.BlockSpec((seq, HEAD_DIM), lambda b, h, i: (b, group * N_HEADS + h))

    def o_spec(tq):
        nq = seq // tq
        return pl.BlockSpec((tq, HEAD_DIM), lambda b, h, i: (b * nq + i, h))

    o_shape = jax.ShapeDtypeStruct((m, HEAD_W), F32)
    sem = ("parallel", "parallel", "arbitrary")
    tq = min(SB_Q_BLOCK, seq)
    o_sb = pl.pallas_call(
        functools.partial(_sb_prompt_body, tq=tq),
        grid=(batch, N_HEADS, seq // tq),
        in_specs=[q_spec(0, tq), kv_spec(1), kv_spec(2)],
        out_specs=o_spec(tq), out_shape=o_shape,
        compiler_params=_params(sem), name="sb_prompt",
    )(qkv16, qkv16, qkv16)
    tq = min(FOX_Q_BLOCK, seq)
    nq = seq // tq
    o_fx = pl.pallas_call(
        functools.partial(_fox_prompt_body, tq=tq, nq=nq),
        grid=(batch, N_HEADS, nq),
        in_specs=[q_spec(3, tq), kv_spec(4), kv_spec(5),
                  pl.BlockSpec((tq, HEAD_DIM), lambda b, h, i: (b * nq + i, 0)),
                  pl.BlockSpec((None, N_HEADS, nq, tq), lambda b, h, i: (b, 0, 0, 0))],
        out_specs=o_spec(tq), out_shape=o_shape,
        scratch_shapes=[pltpu.VMEM((tq, HEAD_DIM), F32), pltpu.VMEM((tq, 1), F32), pltpu.VMEM((tq, 1), F32),
                        pltpu.VMEM((1, 1), F32)],
        compiler_params=_params(sem), name="fox_prompt",
    )(qkv16, qkv16, qkv16, fq, frow.reshape(batch, N_HEADS, nq, tq))
    return o_sb, o_fx


Q_PAD = 8
N_PAIRS = N_HEADS // 2
PAIR_W = 2 * HEAD_DIM


def _expand_heads(x):
    n = x.shape[1]
    return jnp.concatenate([jnp.broadcast_to(x[h:h + 1, :], (Q_PAD, n)) for h in range(N_HEADS)], axis=0)


SB_LIVE = 0
SB_FETCHED = 1
FV_STARTED = 3
N_FLAGS = 8
KEY_SLOTS = 3


def _sample_body(pt_ref, qs_ref, qf_ref, ksn_ref, vsn_ref, kfn_ref, vfn_ref, lfr_ref, tri_ref,
                 csk_hbm, csv_hbm, cfk_hbm, cfv_hbm, clf_hbm, osb_ref, ofx_ref,
                 qbs_ref, qbf_ref, accs_ref, accf_ref, cs_ref, m_ref, l_ref, cg_ref, cq_ref, pad_ref,
                 skb, svb, fkb, fvb, lfb, pr_ref, alpha_ref, flags,
                 sk_sem, sv_sem, fk_sem, fv_sem, lf_sem, *, n_new, pages_per_step, page, chunk):
    g = pages_per_step
    rows = N_HEADS * Q_PAD
    s = pl.program_id(0)
    p = pl.program_id(1)
    n_seq = pl.num_programs(0)
    n_steps = pl.num_programs(1)
    cur = lax.rem(s * n_steps + p, 2)
    nxt = 1 - cur
    last = p == n_steps - 1

    def copies(hbm, buf, sem, seq, step, slot):
        return [pltpu.make_async_copy(hbm.at[pt_ref[seq, (n_steps - 1 - step) * g + gi]],
                                      buf.at[slot, gi], sem.at[slot]) for gi in range(g)]

    def start(*stream):
        for cp in copies(*stream):
            cp.start()

    def wait(*stream):
        for cp in copies(*stream):
            cp.wait()

    sk, sv, fk, fv, lf_ = ((csk_hbm, skb, sk_sem), (csv_hbm, svb, sv_sem), (cfk_hbm, fkb, fk_sem),
                           (cfv_hbm, fvb, fv_sem), (clf_hbm, lfb, lf_sem))

    t = s * n_steps + p
    key_cur = lax.rem(t, KEY_SLOTS)

    @pl.when(t == 0)
    def _cold_start():
        for stream in (fk, lf_, sk, sv):
            start(*stream, 0, 0, 0)
        for i in range(N_FLAGS):
            flags[i] = 0
        flags[SB_FETCHED] = 1

    s_next = jnp.where(last, s + 1, s)
    p_next = jnp.where(last, 0, p + 1)

    @pl.when(jnp.logical_and(t == 0, s_next < n_seq))
    def _cold_start_second_group():
        start(*fk, s_next, p_next, 1)
        start(*lf_, s_next, p_next, 1)

    s_ahead = lax.div(t + 2, n_steps)
    p_ahead = lax.rem(t + 2, n_steps)

    @pl.when(s_ahead < n_seq)
    def _prefetch_keys():
        start(*fk, s_ahead, p_ahead, lax.rem(t + 2, KEY_SLOTS))
        start(*lf_, s_ahead, p_ahead, lax.rem(t + 2, KEY_SLOTS))

    @pl.when(s_next < n_seq)
    def _prefetch():
        fetch_sb = jnp.logical_or(jnp.logical_or(last, p == 0), flags[SB_LIVE] != 0)
        flags[SB_FETCHED + nxt] = fetch_sb.astype(jnp.int32)

        @pl.when(fetch_sb)
        def _():
            start(*sk, s_next, p_next, nxt)
            start(*sv, s_next, p_next, nxt)

    def pair_rows(x, pp):
        return x[pp * 2 * Q_PAD:(pp + 1) * 2 * Q_PAD]

    def scores(qb_ref, key_pair):
        return jnp.concatenate(
            [_nt_dot(qb_ref[pp], key_pair(pp)) for pp in range(N_PAIRS)], axis=0) * QK_SCALE

    def cache_pair(buf, slot, page_slots):
        def head_rows(gi, h):
            return buf[slot, gi, pl.ds(h, page, stride=N_HEADS), :]

        def build(pp):
            return jnp.concatenate(
                [jnp.concatenate([head_rows(gi, 2 * pp), head_rows(gi, 2 * pp + 1)], axis=1)
                 for gi in page_slots], axis=0).astype(BF16)
        return build

    @pl.when(p == 0)
    def _init():
        zeros = jnp.zeros((Q_PAD, HEAD_DIM), F32)

        def block_diag(q_ref_, qb_ref):
            for pp in range(N_PAIRS):
                top = jnp.concatenate([q_ref_[(2 * pp) * Q_PAD:(2 * pp + 1) * Q_PAD, :], zeros], axis=1)
                bot = jnp.concatenate([zeros, q_ref_[(2 * pp + 1) * Q_PAD:(2 * pp + 2) * Q_PAD, :]], axis=1)
                qb_ref[pp] = jnp.concatenate([top, bot], axis=0).astype(BF16)

        block_diag(qs_ref, qbs_ref)
        block_diag(qf_ref, qbf_ref)

        kw = pad_ref.shape[0]
        qi = lax.broadcasted_iota(jnp.int32, (rows, kw), 0) & (Q_PAD - 1)
        ki = lax.broadcasted_iota(jnp.int32, (rows, kw), 1)
        tri = tri_ref[0:kw, 0:kw]

        def new_pair(ref):
            def build(pp):
                pad_ref[...] = jnp.zeros_like(pad_ref)
                pad_ref[0:n_new, 0:HEAD_DIM] = ref[2 * pp]
                pad_ref[0:n_new, HEAD_DIM:PAIR_W] = ref[2 * pp + 1]
                return pad_ref[...].astype(BF16)
            return build

        mask = ki < qi
        z = scores(qbs_ref, new_pair(ksn_ref))
        lsn = _log_sigmoid(-z)
        lk = jnp.where(mask, lsn, 0.0)
        later = _rev_excl_cumsum(lk, tri)
        a = jnp.where(mask, jnp.exp(z + lsn + later), 0.0).astype(BF16)
        v_pair = new_pair(vsn_ref)
        for pp in range(N_PAIRS):
            accs_ref[pp] = _dot(pair_rows(a, pp), v_pair(pp))
        cs_ref[...] = jnp.sum(lk, axis=-1, keepdims=True)

        lfr = lfr_ref[...]
        ck = jnp.sum(lfr, axis=-1, keepdims=True) - _rev_excl_cumsum(lfr, tri)
        ck = _expand_heads(ck)
        cq = jnp.sum(jnp.where(ki == qi, ck, 0.0), axis=-1, keepdims=True)
        cq_ref[...] = cq
        logits = scores(qbf_ref, new_pair(kfn_ref)) + (cq - ck)
        logits = jnp.where(ki <= qi, logits, NEG_BIG)
        m0 = jnp.max(logits, axis=-1, keepdims=True)
        pr = jnp.exp(logits - m0)
        m_ref[...] = m0
        l_ref[...] = jnp.sum(pr, axis=-1, keepdims=True)
        pr = pr.astype(BF16)
        v_pair = new_pair(vfn_ref)
        for pp in range(N_PAIRS):
            accf_ref[pp] = _dot(pair_rows(pr, pp), v_pair(pp))
        cg_ref[...] = jnp.zeros_like(cg_ref)
        flags[SB_LIVE] = 1

    tri = tri_ref[...]
    pages_per_chunk = chunk // page
    n_chunks = (g * page) // chunk
    all_pages = range(g)

    def chunk_pages(ci):
        return range(ci * pages_per_chunk, (ci + 1) * pages_per_chunk)

    wait(*fk, s, p, key_cur)
    wait(*lf_, s, p, key_cur)
    parts = [None] * n_chunks
    for ci in reversed(range(n_chunks)):
        lf = jnp.concatenate([lfb[key_cur, gi] for gi in chunk_pages(ci)], axis=1)
        gl = _rev_excl_cumsum(lf, tri) + cg_ref[...]
        parts[ci] = (scores(qbf_ref, cache_pair(fkb, key_cur, chunk_pages(ci)))
                     + (_expand_heads(gl) + cq_ref[...]))
        cg_ref[...] += jnp.sum(lf, axis=-1, keepdims=True)
    logits = jnp.concatenate(parts, axis=1)
    m_old = m_ref[...]
    row_max = jnp.max(logits, axis=-1, keepdims=True)
    m_new = jnp.maximum(m_old, row_max)
    alpha = jnp.exp(m_old - m_new)
    pr = jnp.exp(logits - m_new)
    m_ref[...] = m_new
    l_ref[...] = l_ref[...] * alpha + jnp.sum(pr, axis=-1, keepdims=True)
    need_values = jnp.max(row_max - m_new) > EXP_ZERO_BELOW

    def apply_values(slot):
        a_slot = alpha_ref[slot]
        pr_slot = pr_ref[slot]
        v_pair = cache_pair(fvb, slot, all_pages)
        for pp in range(N_PAIRS):
            accf_ref[pp] = accf_ref[pp] * pair_rows(a_slot, pp) + _dot(pair_rows(pr_slot, pp), v_pair(pp))

    @pl.when(jnp.logical_and(p > 0, flags[FV_STARTED + nxt] != 0))
    def _previous_group_values():
        wait(*fv, s, p - 1, nxt)
        apply_values(nxt)
        flags[FV_STARTED + nxt] = 0

    @pl.when(need_values)
    def _request_values():
        pr_ref[cur] = pr.astype(BF16)
        alpha_ref[cur] = alpha
        start(*fv, s, p, cur)
        flags[FV_STARTED + cur] = 1

    @pl.when(flags[SB_FETCHED + cur] != 0)
    def _():
        wait(*sk, s, p, cur)
        wait(*sv, s, p, cur)

    @pl.when(jnp.logical_and(flags[SB_FETCHED + cur] != 0, flags[SB_LIVE] != 0))
    def _stick_breaking():
        for ci in reversed(range(n_chunks)):
            z = scores(qbs_ref, cache_pair(skb, cur, chunk_pages(ci)))
            lsn = _log_sigmoid(-z)
            later = _rev_excl_cumsum(lsn, tri) + cs_ref[...]
            a = jnp.exp(z + lsn + later).astype(BF16)
            v_pair = cache_pair(svb, cur, chunk_pages(ci))
            for pp in range(N_PAIRS):
                accs_ref[pp] += _dot(pair_rows(a, pp), v_pair(pp))
            cs_ref[...] += jnp.sum(lsn, axis=-1, keepdims=True)
        real_row = (lax.broadcasted_iota(jnp.int32, (rows, 1), 0) & (Q_PAD - 1)) < n_new
        flags[SB_LIVE] = (jnp.max(jnp.where(real_row, cs_ref[...], NEG_BIG)) > EXP_ZERO_BELOW).astype(jnp.int32)

    @pl.when(last)
    def _fin():
        @pl.when(flags[FV_STARTED + cur] != 0)
        def _():
            wait(*fv, s, p, cur)
            apply_values(cur)
            flags[FV_STARTED + cur] = 0

        inv_l = 1.0 / l_ref[...]
        for h in range(N_HEADS):
            pp, odd = divmod(h, 2)
            r0, c0 = odd * Q_PAD, odd * HEAD_DIM
            osb_ref[h * Q_PAD:(h + 1) * Q_PAD, :] = accs_ref[pp, r0:r0 + Q_PAD, c0:c0 + HEAD_DIM]
            ofx_ref[h * Q_PAD:(h + 1) * Q_PAD, :] = (accf_ref[pp, r0:r0 + Q_PAD, c0:c0 + HEAD_DIM]
                                                     * inv_l[h * Q_PAD:(h + 1) * Q_PAD, :])


def _sample_attention(page_table, q_sb, q_fx, k_sb, v_sb, k_fx, v_fx, lf_row,
                      c_sb_k, c_sb_v, c_fx_k, c_fx_v, c_lf_t, *, pages_per_step):
    n_seq, _, n_new, _ = k_sb.shape
    n_pages = page_table.shape[1]
    page = c_sb_k.shape[1] // N_HEADS
    g = pages_per_step
    assert n_pages % g == 0 and n_new <= Q_PAD
    n_steps = n_pages // g
    rows = N_HEADS * Q_PAD
    chunk = min(2 * page, g * page)
    assert (g * page) % chunk == 0
    tri = _upper_strict(chunk)

    def seq_spec(shape):
        return pl.BlockSpec((None,) + shape, lambda s, p, pt: (s,) + (0,) * len(shape))

    in_specs = ([seq_spec((rows, HEAD_DIM))] * 2 + [seq_spec((N_HEADS, n_new, HEAD_DIM))] * 4
                + [seq_spec((N_HEADS, HEAD_DIM)), pl.BlockSpec((chunk, chunk), lambda s, p, pt: (0, 0))]
                + [pl.BlockSpec(memory_space=pl.ANY)] * 5)
    args = [q_sb, q_fx, k_sb, v_sb, k_fx, v_fx, lf_row, tri, c_sb_k, c_sb_v, c_fx_k, c_fx_v, c_lf_t]
    page_buf = pltpu.VMEM((2, g, page * N_HEADS, HEAD_DIM), F32)
    key_buf = pltpu.VMEM((KEY_SLOTS, g, page * N_HEADS, HEAD_DIM), F32)
    dma_sem = pltpu.SemaphoreType.DMA((2,))
    key_sem = pltpu.SemaphoreType.DMA((KEY_SLOTS,))

    grid_spec = pltpu.PrefetchScalarGridSpec(
        num_scalar_prefetch=1,
        grid=(n_seq, n_steps),
        in_specs=in_specs,
        out_specs=[seq_spec((rows, HEAD_DIM))] * 2,
        scratch_shapes=[
            pltpu.VMEM((N_PAIRS, 2 * Q_PAD, PAIR_W), BF16), pltpu.VMEM((N_PAIRS, 2 * Q_PAD, PAIR_W), BF16),
            pltpu.VMEM((N_PAIRS, 2 * Q_PAD, PAIR_W), F32), pltpu.VMEM((N_PAIRS, 2 * Q_PAD, PAIR_W), F32),
            pltpu.VMEM((rows, 1), F32), pltpu.VMEM((rows, 1), F32), pltpu.VMEM((rows, 1), F32),
            pltpu.VMEM((N_HEADS, 1), F32), pltpu.VMEM((rows, 1), F32),
            pltpu.VMEM((HEAD_DIM, PAIR_W), F32),
            page_buf, page_buf, key_buf, page_buf, pltpu.VMEM((KEY_SLOTS, g, N_HEADS, page), F32),
            pltpu.VMEM((2, rows, g * page), BF16), pltpu.VMEM((2, rows, 1), F32),
            pltpu.SMEM((N_FLAGS,), jnp.int32),
            dma_sem, dma_sem, key_sem, dma_sem, key_sem,
        ],
    )
    return pl.pallas_call(
        functools.partial(_sample_body, n_new=n_new, pages_per_step=g, page=page, chunk=chunk),
        grid_spec=grid_spec,
        out_shape=[jax.ShapeDtypeStruct((n_seq, rows, HEAD_DIM), F32)] * 2,
        compiler_params=_params(("arbitrary", "arbitrary")),
        name="sample_attn",
    )(page_table, *args)


def _merge_body(osb_ref, ofx_ref, zsb_ref, zfx_ref, gsb_ref, gfx_ref, bm_ref, wsb_ref, wfx_ref, out_ref):
    def branch(o_ref, z_ref, w_ref):
        z = z_ref[...]
        a = o_ref[...] * (z * jax.nn.sigmoid(z))
        return _dot(a.astype(BF16), w_ref[...])

    bm = bm_ref[...]
    merged = (jax.nn.sigmoid(gsb_ref[...] + bm[0:1, :]) * branch(osb_ref, zsb_ref, wsb_ref)
              + jax.nn.sigmoid(gfx_ref[...] + bm[1:2, :]) * branch(ofx_ref, zfx_ref, wfx_ref))
    out_ref[...] = merged.astype(BF16)


def _merge(o_sb, o_fx, rest32, b_merge, w_br_sb16, w_br_fox16, d_model):
    m = o_sb.shape[0]
    tm = _pick_tile(m, 512)
    tn = _pick_tile(math.gcd(d_model, HEAD_W), 1024)
    g_off = 2 * HEAD_W // tn
    nd = d_model // tn
    row_spec = pl.BlockSpec((tm, HEAD_W), lambda i, j: (i, 0))
    return pl.pallas_call(
        _merge_body,
        grid=(m // tm, nd),
        in_specs=[row_spec, row_spec,
                  pl.BlockSpec((tm, HEAD_W), lambda i, j: (i, 0)),
                  pl.BlockSpec((tm, HEAD_W), lambda i, j: (i, 1)),
                  pl.BlockSpec((tm, tn), lambda i, j: (i, g_off + j)),
                  pl.BlockSpec((tm, tn), lambda i, j: (i, g_off + nd + j)),
                  pl.BlockSpec((2, tn), lambda i, j: (0, j)),
                  pl.BlockSpec((HEAD_W, tn), lambda i, j: (0, j)),
                  pl.BlockSpec((HEAD_W, tn), lambda i, j: (0, j))],
        out_specs=pl.BlockSpec((tm, tn), lambda i, j: (i, j)),
        out_shape=jax.ShapeDtypeStruct((m, d_model), BF16),
        compiler_params=_params(("parallel", "arbitrary")),
        name="merge",
    )(o_sb, o_fx, rest32, rest32, rest32, rest32, b_merge, w_br_sb16, w_br_fox16)


def _mix_ln_body(x_ref, gi_ref, bi_ref, mg_ref, wo_ref, g_ref, b_ref, out_ref, *, alpha):
    h = _layer_norm(x_ref[...], gi_ref[...], bi_ref[...])
    y = _dot(mg_ref[...], wo_ref[...])
    out_ref[...] = _layer_norm(alpha * h + y, g_ref[...], b_ref[...])


def _mix_ln(x, ln_in_g, ln_in_b, merged16, w_o16, g, b, alpha):
    m, d = x.shape
    tm = _pick_tile(m, 256)
    row = pl.BlockSpec((tm, d), lambda i: (i, 0))
    vec = pl.BlockSpec((1, d), lambda i: (0, 0))
    return pl.pallas_call(
        functools.partial(_mix_ln_body, alpha=alpha),
        grid=(m // tm,),
        in_specs=[row, vec, vec, row, pl.BlockSpec((d, d), lambda i: (0, 0)), vec, vec],
        out_specs=row,
        out_shape=jax.ShapeDtypeStruct((m, d), F32),
        compiler_params=_params(("parallel",)),
        name="mix_ln",
    )(x, ln_in_g.reshape(1, d), ln_in_b.reshape(1, d), merged16, w_o16, g.reshape(1, d), b.reshape(1, d))


def _ple_body(h_ref, p_ref, wpg_ref, bpg_ref, wpe_ref, g_ref, b_ref, out_ref, *, alpha):
    h = h_ref[...]
    gate = jax.nn.sigmoid(_dot(h.astype(BF16), wpg_ref[...]) + bpg_ref[...])
    e = _dot(p_ref[...].astype(BF16), wpe_ref[...])
    out_ref[...] = _layer_norm(alpha * h + gate * e, g_ref[...], b_ref[...])


def _ple(h, p, w_pg16, b_pg, w_pe16, g, b, alpha):
    m, d = h.shape
    dp = p.shape[1]
    tm = _pick_tile(m, 256)
    row = pl.BlockSpec((tm, d), lambda i: (i, 0))
    vec = pl.BlockSpec((1, d), lambda i: (0, 0))
    return pl.pallas_call(
        functools.partial(_ple_body, alpha=alpha),
        grid=(m // tm,),
        in_specs=[row, pl.BlockSpec((tm, dp), lambda i: (i, 0)),
                  pl.BlockSpec((d, d), lambda i: (0, 0)), vec,
                  pl.BlockSpec((dp, d), lambda i: (0, 0)), vec, vec],
        out_specs=row,
        out_shape=jax.ShapeDtypeStruct((m, d), F32),
        compiler_params=_params(("parallel",)),
        name="ple",
    )(h, p, w_pg16, b_pg.reshape(1, d), w_pe16, g.reshape(1, d), b.reshape(1, d))


def kernel(x_prompt, x_sample, cache_sb_k, cache_sb_v, cache_fox_k, cache_fox_v, cache_fox_logf, page_table, p_prompt, p_sample, ln_in_g, ln_in_b, w_in, b_f, b_merge, w_br_sb, w_br_fox, w_o, ln_mix_g, ln_mix_b, w_pe, w_pg, b_pg, ln_ple_g, ln_ple_b):
    depth = w_in.shape[0]
    assert depth == 1, "single-layer trunk only"
    batch, seq, d_model = x_prompt.shape
    n_seq, n_new, _ = x_sample.shape
    alpha = (2.0 * depth) ** 0.25
    assert w_in.shape[2] == 8 * HEAD_W + N_HEADS + 2 * d_model

    w = w_in[0]

    def cols16(lo, hi):
        return w[:, lo:hi].astype(BF16)

    qkv_cols = jnp.concatenate([cols16(0, 3 * HEAD_W), cols16(4 * HEAD_W, 7 * HEAD_W)], axis=1)
    rest_cols = jnp.concatenate([cols16(3 * HEAD_W, 4 * HEAD_W), cols16(7 * HEAD_W, 8 * HEAD_W),
                                 cols16(8 * HEAD_W + N_HEADS, w.shape[1])], axis=1)
    wf16 = jnp.pad(cols16(8 * HEAD_W, 8 * HEAD_W + N_HEADS), ((0, 0), (0, HEAD_DIM - N_HEADS)))
    bf_pad = jnp.pad(b_f[0], (0, HEAD_DIM - N_HEADS)).reshape(1, HEAD_DIM)
    w_br_sb16 = w_br_sb[0].astype(BF16)
    w_br_fox16 = w_br_fox[0].astype(BF16)
    w_o16 = w_o[0].astype(BF16)
    w_pg16 = w_pg[0].astype(BF16)
    w_pe16 = w_pe[0].astype(BF16)

    def project(x2d):
        qkv16, *kv = _ln_proj(x2d, ln_in_g, ln_in_b, qkv_cols, tm_limit=512, emit32=False, emit16=True,
                              head_tiles=(1, 2, 4, 5))
        rest32, logf = _ln_proj(x2d, ln_in_g, ln_in_b, rest_cols, tm_limit=1024, emit32=True, emit16=False,
                                wf16=wf16, bf=bf_pad)
        return qkv16, kv, rest32, logf

    def finish(x2d, o_sb, o_fx, rest32, p2d):
        merged16 = _merge(o_sb, o_fx, rest32, b_merge[0], w_br_sb16, w_br_fox16, d_model)
        h = _mix_ln(x2d, ln_in_g, ln_in_b, merged16, w_o16, ln_mix_g[0], ln_mix_b[0], alpha)
        return _ple(h, p2d, w_pg16, b_pg[0], w_pe16, ln_ple_g[0], ln_ple_b[0], alpha)

    def new_rows(kv, logf, lead):
        return ([a.reshape(1, *lead, N_HEADS, HEAD_DIM) for a in kv]
                + [logf[:, :N_HEADS].reshape(1, *lead, N_HEADS)])

    xp = x_prompt.reshape(batch * seq, d_model)
    qkv16_p, kv_p, rest32_p, logf_p = project(xp)
    fq, frow = _fcum(logf_p, batch, seq)
    o_sb_p, o_fx_p = _prompt_attention(qkv16_p, fq, frow, batch, seq)
    y_prompt = finish(xp, o_sb_p, o_fx_p, rest32_p, p_prompt[0].reshape(batch * seq, -1))

    xs = x_sample.reshape(n_seq * n_new, d_model)
    qkv16_s, kv_s, rest32_s, logf_s = project(xs)

    def heads_major(a):
        return jnp.swapaxes(a.reshape(n_seq, n_new, N_HEADS, HEAD_DIM), 1, 2)

    def query_rows(c):
        q = heads_major(qkv16_s[:, c * HEAD_W:(c + 1) * HEAD_W].astype(F32))
        q = jnp.pad(q, ((0, 0), (0, 0), (0, Q_PAD - n_new), (0, 0)))
        return q.reshape(n_seq, N_HEADS * Q_PAD, HEAD_DIM)

    def token_rows(o):
        o = o.reshape(n_seq, N_HEADS, Q_PAD, HEAD_DIM)[:, :, :n_new]
        return jnp.swapaxes(o, 1, 2).reshape(n_seq * n_new, HEAD_W)

    def key_head_rows(cache):
        n_pool, page = cache.shape[1:3]
        return cache[0].reshape(n_pool, page * N_HEADS, HEAD_DIM)

    lf_new = logf_s[:, :N_HEADS].reshape(n_seq, n_new, N_HEADS)
    lf_row = jnp.pad(jnp.swapaxes(lf_new, 1, 2), ((0, 0), (0, 0), (0, HEAD_DIM - n_new)))
    o_sb_s, o_fx_s = _sample_attention(
        page_table, query_rows(0), query_rows(3), *[heads_major(a) for a in kv_s],
        lf_row, key_head_rows(cache_sb_k), key_head_rows(cache_sb_v), key_head_rows(cache_fox_k),
        key_head_rows(cache_fox_v), jnp.swapaxes(cache_fox_logf[0], 1, 2),
        pages_per_step=math.gcd(page_table.shape[1], 8))
    y_sample = finish(xs, token_rows(o_sb_s), token_rows(o_fx_s), rest32_s, p_sample[0].reshape(n_seq * n_new, -1))

    return (y_prompt.reshape(batch, seq, d_model), y_sample.reshape(n_seq, n_new, d_model),
            *new_rows(kv_p, logf_p, (batch, seq)), *new_rows(kv_s, logf_s, (n_seq, n_new)))
```
